```python
import math, functools
import jax, jax.numpy as jnp
from jax import lax
import numpy as np

D_MODEL = 1024
BATCH = 4
SEQ = 4096
DEPTH = 4
DEC_BATCH = 128
DEC_SEQ = 8
PAST_LEN = 2048
PAGE_SIZE = 128

N_HEADS = 8
N_KV = 2
GROUP = N_HEADS // N_KV
HEAD_DIM = 64
ATTN_W = N_HEADS * HEAD_DIM
KV_W = N_KV * HEAD_DIM
CMP_LEN = 32
CMP_STRIDE = 16
SEL_LEN = 64
N_SEL = 8
WINDOW = 512
Q_BLOCK = 128
FORCE_BONUS = 1000.0
CONV_CH = D_MODEL // 2
CONV_WIDTH = 31
D_FF = 2816
FFN_CONV_WIDTH = 3
IN_W = ATTN_W + 6 * KV_W + 3 * N_HEADS + 2 * CONV_CH + 2 * D_MODEL
EPS = 1e-6
NEG = -1e30
TINY = 1e-30

kernel_name = "nsa_conformer_gated_hybrid_step"


def rmsnorm(x, g):
    x32 = x.astype(jnp.float32)
    y = x32 * lax.rsqrt(jnp.mean(x32 * x32, -1, keepdims=True) + EPS) * g.astype(jnp.float32)
    return y.astype(x.dtype)


def layernorm(x, g, b):
    x32 = x.astype(jnp.float32)
    mu = jnp.mean(x32, -1, keepdims=True)
    xc = x32 - mu
    var = jnp.mean(xc * xc, -1, keepdims=True)
    return (xc * lax.rsqrt(var + EPS) * g.astype(jnp.float32) + b.astype(jnp.float32)).astype(x.dtype)


def masked_softmax(s, mask):
    s = jnp.where(mask, s.astype(jnp.float32), NEG)
    m = jnp.max(s, -1, keepdims=True)
    e = jnp.where(mask, jnp.exp(s - m), 0.0)
    return e / jnp.maximum(jnp.sum(e, -1, keepdims=True), TINY)


def causal_dwconv(u_ext, w, b):
    C = u_ext.shape[-1]
    y = lax.conv_general_dilated(u_ext, w[:, None, :], (1,), 'VALID',
                                 dimension_numbers=('NWC', 'WIO', 'NWC'), feature_group_count=C)
    return y + b


def compress_blocks(k, w, b):
    B, T, G, HD = k.shape
    n_cmp = (T - CMP_LEN) // CMP_STRIDE + 1
    idx = (jnp.arange(n_cmp) * CMP_STRIDE)[:, None] + jnp.arange(CMP_LEN)[None, :]
    blk = jnp.transpose(k[:, idx], (0, 1, 3, 2, 4)).reshape(B, n_cmp, G, CMP_LEN * HD)
    return blk @ w + b


def sel_blocks(k):
    B, T, G, HD = k.shape
    n_slc = -(-T // SEL_LEN)
    k = jnp.pad(k, ((0, 0), (0, n_slc * SEL_LEN - T), (0, 0), (0, 0)))
    return jnp.transpose(k.reshape(B, n_slc, SEL_LEN, G, HD), (0, 3, 1, 2, 4))


def cmp_to_sel_map(n_cmp, n_slc):
    s = jnp.arange(n_cmp) * CMP_STRIDE
    e = s + CMP_LEN - 1
    js = jnp.arange(n_slc) * SEL_LEN
    je = js + SEL_LEN - 1
    return ((s[:, None] <= je[None, :]) & (e[:, None] >= js[None, :])).astype(jnp.float32)


def nsa_core(q, t, gates, kcmp, vcmp, cmp_end, sel_map, ks_blk, vs_blk, kw, vw, kw_pos):
    B, Tq = q.shape[:2]
    q = q * (HEAD_DIM ** -0.5)
    s_c = jnp.einsum('bqgrd,bngd->bqgrn', q, kcmp)
    p_c = masked_softmax(s_c, (cmp_end[None, :] <= t[:, None])[None, :, None, None, :])
    o_c = jnp.einsum('bqgrn,bngd->bqgrd', p_c.astype(vcmp.dtype), vcmp)
    imp = jnp.einsum('bqgrn,nj->bqgj', p_c, sel_map)
    n_slc = sel_map.shape[1]
    jb = jnp.arange(n_slc)
    cur = (t // SEL_LEN)[:, None]
    valid = (jb[None, :] * SEL_LEN) <= t[:, None]
    forced = (jb[None, :] == 0) | (jb[None, :] == cur) | (jb[None, :] == cur - 1)
    score = jnp.where(valid[None, :, None, :],
                      imp + FORCE_BONUS * forced[None, :, None, :].astype(jnp.float32), -1.0)
    n_top = min(N_SEL, n_slc)
    _, sel = lax.top_k(score, n_top)
    bi = jnp.arange(B)[:, None, None, None]
    gi = jnp.arange(N_KV)[None, None, :, None]
    k_s = ks_blk[bi, gi, sel]
    v_s = vs_blk[bi, gi, sel]
    kpos = sel[..., None] * SEL_LEN + jnp.arange(SEL_LEN)
    mask_s = (kpos <= t[None, :, None, None, None]).reshape(B, Tq, N_KV, 1, n_top * SEL_LEN)
    s_s = jnp.einsum('bqgrd,bqgnkd->bqgrnk', q, k_s).reshape(B, Tq, N_KV, GROUP, n_top * SEL_LEN)
    p_s = masked_softmax(s_s, mask_s).reshape(B, Tq, N_KV, GROUP, n_top, SEL_LEN)
    o_s = jnp.einsum('bqgrnk,bqgnkd->bqgrd', p_s.astype(v_s.dtype), v_s)
    dt = t[:, None] - kw_pos[None, :]
    mask_w = ((dt >= 0) & (dt <= WINDOW) & (kw_pos >= 0)[None, :])[None, :, None, None, :]
    s_w = jnp.einsum('bqgrd,bkgd->bqgrk', q, kw)
    p_w = masked_softmax(s_w, mask_w)
    o_w = jnp.einsum('bqgrk,bkgd->bqgrd', p_w.astype(vw.dtype), vw)
    return (gates[:, :, 0][..., None] * o_c + gates[:, :, 1][..., None] * o_s
            + gates[:, :, 2][..., None] * o_w)


def nsa_prompt(q, gates, kc, vc, ks, vs, kw, vw, phi):
    w_pk, b_pk, w_pv, b_pv = phi
    B, S = q.shape[:2]
    kcmp = compress_blocks(kc, w_pk, b_pk)
    vcmp = compress_blocks(vc, w_pv, b_pv)
    n_cmp = kcmp.shape[1]
    cmp_end = jnp.arange(n_cmp) * CMP_STRIDE + (CMP_LEN - 1)
    ks_blk = sel_blocks(ks)
    vs_blk = sel_blocks(vs)
    sel_map = cmp_to_sel_map(n_cmp, ks_blk.shape[2])
    kw_pad = jnp.pad(kw, ((0, 0), (WINDOW, 0), (0, 0), (0, 0)))
    vw_pad = jnp.pad(vw, ((0, 0), (WINDOW, 0), (0, 0), (0, 0)))
    nb = S // Q_BLOCK
    qb = q.reshape(B, nb, Q_BLOCK, N_KV, GROUP, HEAD_DIM).swapaxes(0, 1)
    gb = gates.reshape(B, nb, Q_BLOCK, 3, N_KV, GROUP).swapaxes(0, 1)

    def one_block(args):
        q_i, g_i, i = args
        start = i * Q_BLOCK
        t = start + jnp.arange(Q_BLOCK)
        kw_i = lax.dynamic_slice_in_dim(kw_pad, start, WINDOW + Q_BLOCK, axis=1)
        vw_i = lax.dynamic_slice_in_dim(vw_pad, start, WINDOW + Q_BLOCK, axis=1)
        kw_pos = start - WINDOW + jnp.arange(WINDOW + Q_BLOCK)
        return nsa_core(q_i, t, g_i, kcmp, vcmp, cmp_end, sel_map, ks_blk, vs_blk, kw_i, vw_i, kw_pos)

    o = lax.map(one_block, (qb, gb, jnp.arange(nb)))
    o = o.swapaxes(0, 1).reshape(B, S, ATTN_W)
    win = jnp.stack([kw, vw], 2)[:, -WINDOW:]
    return o, win


def nsa_sample(q, gates, kc, vc, ks, vs, kw, vw, phi, past, win_buf):
    w_pk, b_pk, w_pv, b_pv = phi
    P = past.shape[1]
    WB = win_buf.shape[1]
    Tq = q.shape[1]
    kc_all = jnp.concatenate([past[:, :, 0], kc], 1)
    vc_all = jnp.concatenate([past[:, :, 1], vc], 1)
    ks_all = jnp.concatenate([past[:, :, 2], ks], 1)
    vs_all = jnp.concatenate([past[:, :, 3], vs], 1)
    kcmp = compress_blocks(kc_all, w_pk, b_pk)
    vcmp = compress_blocks(vc_all, w_pv, b_pv)
    n_cmp = kcmp.shape[1]
    cmp_end = jnp.arange(n_cmp) * CMP_STRIDE + (CMP_LEN - 1)
    ks_blk = sel_blocks(ks_all)
    vs_blk = sel_blocks(vs_all)
    sel_map = cmp_to_sel_map(n_cmp, ks_blk.shape[2])
    new_w = jnp.stack([kw, vw], 2)
    win_all = jnp.concatenate([win_buf, new_w], 1)
    kw_pos = P - WB + jnp.arange(WB + Tq)
    t = P + jnp.arange(Tq)
    o = nsa_core(q, t, gates, kcmp, vcmp, cmp_end, sel_map, ks_blk, vs_blk,
                 win_all[:, :, 0], win_all[:, :, 1], kw_pos)
    return o.reshape(q.shape[0], Tq, ATTN_W), win_all[:, -WINDOW:]


def mixer_inputs(z, w_in):
    B, T = z.shape[:2]
    sizes = [ATTN_W] + [KV_W] * 6 + [3 * N_HEADS, 2 * CONV_CH]
    cuts = [int(c) for c in np.cumsum(sizes)]
    pieces = jnp.split(z @ w_in, cuts, axis=-1)
    q = pieces[0].reshape(B, T, N_KV, GROUP, HEAD_DIM)
    kc, vc, ks, vs, kw, vw = [p.reshape(B, T, N_KV, HEAD_DIM) for p in pieces[1:7]]
    g_nsa = jax.nn.sigmoid(pieces[7]).reshape(B, T, 3, N_KV, GROUP)
    u2 = pieces[8]
    u = u2[..., :CONV_CH] * jax.nn.sigmoid(u2[..., CONV_CH:])
    g_mrg = jax.nn.sigmoid(pieces[9]).reshape(B, T, 2, D_MODEL)
    return q, kc, vc, ks, vs, kw, vw, g_nsa, u, g_mrg


def conformer_conv(u, hist, conv_w, conv_b, ln_g, ln_b):
    ext = jnp.concatenate([hist, u], 1)
    y = jax.nn.silu(layernorm(causal_dwconv(ext, conv_w, conv_b), ln_g, ln_b))
    return y, ext[:, -(CONV_WIDTH - 1):]


def conv_ffn(z, hist, w_up, cw, cb, w_down):
    ext = jnp.concatenate([hist, z @ w_up], 1)
    h = causal_dwconv(ext, cw, cb)
    g, v = jnp.split(h, 2, axis=-1)
    return (jax.nn.gelu(g, approximate=True) * v) @ w_down, ext[:, -(FFN_CONV_WIDTH - 1):]


def trunk_layer(x, attend, conv_hist, ffn_hist, g_mix_pre, g_mix_post, g_ffn_pre, g_ffn_post, w_in,
                w_attn_out, conv_w, conv_b, conv_ln_g, conv_ln_b, w_conv_out, w_out, w_up,
                ffn_conv_w, ffn_conv_b, w_down):
    z = rmsnorm(x, g_mix_pre)
    q, kc, vc, ks, vs, kw, vw, g_nsa, u, g_mrg = mixer_inputs(z, w_in)
    o_attn, win_state = attend(q, g_nsa, kc, vc, ks, vs, kw, vw)
    o_conv, conv_state = conformer_conv(u, conv_hist, conv_w, conv_b, conv_ln_g, conv_ln_b)
    h = (g_mrg[:, :, 0] * (o_attn @ w_attn_out) + g_mrg[:, :, 1] * (o_conv @ w_conv_out)) @ w_out
    x = x + rmsnorm(h, g_mix_post)
    f, ffn_state = conv_ffn(rmsnorm(x, g_ffn_pre), ffn_hist, w_up, ffn_conv_w, ffn_conv_b, w_down)
    x = x + rmsnorm(f, g_ffn_post)
    kv_rows = jnp.stack([kc, vc, ks, vs], 2)
    return x, kv_rows, win_state, conv_state, ffn_state


def setup_inputs(seed: int = 0) -> dict:
    key = jax.random.key(seed)
    ks = list(jax.random.split(key, 32))
    f32 = jnp.float32

    def nrm(k, shape, scale):
        return jax.random.normal(k, shape, f32) * scale

    n_pages = PAST_LEN // PAGE_SIZE
    n_used = DEC_BATCH * n_pages
    n_phys = n_used + max(1, n_used // 4)
    win_buf = min(WINDOW, PAST_LEN)
    page_table = jax.random.permutation(ks[0], n_phys)[:n_used].reshape(DEC_BATCH, n_pages).astype(jnp.int32)
    return {
        "x_prompt": nrm(ks[1], (BATCH, SEQ, D_MODEL), 1.0),
        "x_sample": nrm(ks[2], (DEC_BATCH, DEC_SEQ, D_MODEL), 1.0),
        "cache_kv": nrm(ks[3], (DEPTH, n_phys, PAGE_SIZE, 4, N_KV, HEAD_DIM), 1.0),
        "state_kv_win": nrm(ks[4], (DEPTH, DEC_BATCH, win_buf, 2, N_KV, HEAD_DIM), 1.0),
        "state_conv": nrm(ks[5], (DEPTH, DEC_BATCH, CONV_WIDTH - 1, CONV_CH), 0.5),
        "state_ffn": nrm(ks[6], (DEPTH, DEC_BATCH, FFN_CONV_WIDTH - 1, 2 * D_FF), 1.0),
        "page_table": page_table,
        "norm_mix_pre": 1.0 + nrm(ks[7], (DEPTH, D_MODEL), 0.05),
        "norm_mix_post": 1.0 + nrm(ks[8], (DEPTH, D_MODEL), 0.05),
        "norm_ffn_pre": 1.0 + nrm(ks[9], (DEPTH, D_MODEL), 0.05),
        "norm_ffn_post": 1.0 + nrm(ks[10], (DEPTH, D_MODEL), 0.05),
        "w_in": nrm(ks[11], (DEPTH, D_MODEL, IN_W), D_MODEL ** -0.5),
        "w_phi_k": nrm(ks[12], (DEPTH, CMP_LEN * HEAD_DIM, HEAD_DIM), (CMP_LEN * HEAD_DIM) ** -0.5),
        "b_phi_k": nrm(ks[13], (DEPTH, HEAD_DIM), 0.02),
        "w_phi_v": nrm(ks[14], (DEPTH, CMP_LEN * HEAD_DIM, HEAD_DIM), (CMP_LEN * HEAD_DIM) ** -0.5),
        "b_phi_v": nrm(ks[15], (DEPTH, HEAD_DIM), 0.02),
        "w_attn_out": nrm(ks[16], (DEPTH, ATTN_W, D_MODEL), ATTN_W ** -0.5),
        "conv_w": nrm(ks[17], (DEPTH, CONV_WIDTH, CONV_CH), CONV_WIDTH ** -0.5),
        "conv_b": nrm(ks[18], (DEPTH, CONV_CH), 0.02),
        "conv_ln_g": 1.0 + nrm(ks[19], (DEPTH, CONV_CH), 0.05),
        "conv_ln_b": nrm(ks[20], (DEPTH, CONV_CH), 0.02),
        "w_conv_out": nrm(ks[21], (DEPTH, CONV_CH, D_MODEL), CONV_CH ** -0.5),
        "w_out": nrm(ks[22], (DEPTH, D_MODEL, D_MODEL), D_MODEL ** -0.5),
        "w_up": nrm(ks[23], (DEPTH, D_MODEL, 2 * D_FF), D_MODEL ** -0.5),
        "ffn_conv_w": nrm(ks[24], (DEPTH, FFN_CONV_WIDTH, 2 * D_FF), FFN_CONV_WIDTH ** -0.5),
        "ffn_conv_b": nrm(ks[25], (DEPTH, 2 * D_FF), 0.02),
        "w_down": nrm(ks[26], (DEPTH, D_FF, D_MODEL), D_FF ** -0.5),
    }


def reference(x_prompt, x_sample, cache_kv, state_kv_win, state_conv, state_ffn, page_table,
              norm_mix_pre, norm_mix_post, norm_ffn_pre, norm_ffn_post, w_in,
              w_phi_k, b_phi_k, w_phi_v, b_phi_v, w_attn_out, conv_w, conv_b, conv_ln_g, conv_ln_b,
              w_conv_out, w_out, w_up, ffn_conv_w, ffn_conv_b, w_down):
    B = x_prompt.shape[0]
    DB = x_sample.shape[0]
    xp, xs = x_prompt, x_sample
    kv_p, kv_s, win_p, win_s, conv_p, conv_s, ffn_p, ffn_s = [], [], [], [], [], [], [], []
    for l in range(DEPTH):
        lw = (norm_mix_pre[l], norm_mix_post[l], norm_ffn_pre[l], norm_ffn_post[l], w_in[l],
              w_attn_out[l], conv_w[l], conv_b[l], conv_ln_g[l], conv_ln_b[l], w_conv_out[l], w_out[l],
              w_up[l], ffn_conv_w[l], ffn_conv_b[l], w_down[l])
        phi = (w_phi_k[l], b_phi_k[l], w_phi_v[l], b_phi_v[l])
        attend_p = functools.partial(nsa_prompt, phi=phi)
        conv_h0 = jnp.zeros((B, CONV_WIDTH - 1, CONV_CH), xp.dtype)
        ffn_h0 = jnp.zeros((B, FFN_CONV_WIDTH - 1, 2 * D_FF), xp.dtype)
        xp, r_kv, r_win, r_conv, r_ffn = trunk_layer(xp, attend_p, conv_h0, ffn_h0, *lw)
        kv_p.append(r_kv); win_p.append(r_win); conv_p.append(r_conv); ffn_p.append(r_ffn)
        past = cache_kv[l][page_table].reshape(DB, -1, 4, N_KV, HEAD_DIM)
        attend_s = functools.partial(nsa_sample, phi=phi, past=past, win_buf=state_kv_win[l])
        xs, r_kv, r_win, r_conv, r_ffn = trunk_layer(xs, attend_s, state_conv[l], state_ffn[l], *lw)
        kv_s.append(r_kv); win_s.append(r_win); conv_s.append(r_conv); ffn_s.append(r_ffn)
    return (xp, xs, jnp.stack(kv_p), jnp.stack(kv_s), jnp.stack(win_p), jnp.stack(win_s),
            jnp.stack(conv_p), jnp.stack(conv_s), jnp.stack(ffn_p), jnp.stack(ffn_s))
```

```python
import functools

import numpy as np
import jax
import jax.numpy as jnp
from jax import lax
from jax.experimental import pallas as pl
from jax.experimental.pallas import tpu as pltpu

F32 = jnp.float32
BF16 = jnp.bfloat16

N_HEADS = 8
N_KV = 2
GROUP = N_HEADS // N_KV
HEAD_DIM = 64
ATTN_W = N_HEADS * HEAD_DIM
KV_W = N_KV * HEAD_DIM
CMP_LEN = 32
CMP_STRIDE = 16
SEL_LEN = 64
N_SEL = 8
WINDOW = 512
Q_BLOCK = 128
FORCE_BONUS = 1000.0
CONV_WIDTH = 31
FFN_CONV_WIDTH = 3
EPS = 1e-6
NEG = -1e30
TINY = 1e-30
GATE_PAD = 128

LANES = 128
SUBLANES = 8
VMEM_LIMIT = 56 * 1024 * 1024


def _cparams(*sem):
    return pltpu.CompilerParams(dimension_semantics=sem, vmem_limit_bytes=VMEM_LIMIT)


def _rms(x, g):
    return x * lax.rsqrt(jnp.mean(x * x, axis=-1, keepdims=True) + EPS) * g


def _sigmoid(x):
    return 1.0 / (1.0 + jnp.exp(-x))


def _gelu_tanh(x):
    c = np.sqrt(2.0 / np.pi).astype(np.float32)
    return x * (0.5 * (1.0 + jnp.tanh(c * (x + 0.044715 * (x * x * x)))))


def _dot(a, b):
    return jnp.dot(a, b, preferred_element_type=F32)


def _dot_nt(a, b):
    return lax.dot_general(a, b, (((1,), (1,)), ((), ())), preferred_element_type=F32)


def _softmax_parts(s, mask):
    s = jnp.where(mask, s, NEG)
    m = jnp.max(s, axis=-1, keepdims=True)
    e = jnp.where(mask, jnp.exp(s - m), 0.0)
    inv = 1.0 / jnp.maximum(jnp.sum(e, axis=-1, keepdims=True), TINY)
    return e, inv


def _build_q2(q, nq):
    lane = lax.broadcasted_iota(jnp.int32, (nq, LANES), 1)
    parts = []
    for h in range(N_HEADS):
        g = h // GROUP
        c = h // 2
        chunk = q[:, c * LANES:(c + 1) * LANES]
        if (h % 2) != g:
            chunk = pltpu.roll(chunk, HEAD_DIM, 1)
        keep = (lane >= g * HEAD_DIM) & (lane < (g + 1) * HEAD_DIM)
        parts.append(jnp.where(keep, chunk * (HEAD_DIM ** -0.5), 0.0))
    return jnp.concatenate(parts, axis=0).astype(BF16)


def _row_positions(nq, t0):
    r = lax.broadcasted_iota(jnp.int32, (N_HEADS * nq, 1), 0)
    return t0 + (r & (nq - 1))


def _select_blocks(p_c, nq, t0, selmap, n_slc):
    ps = []
    for g in range(N_KV):
        acc = p_c[(g * GROUP) * nq:(g * GROUP + 1) * nq]
        for r in range(1, GROUP):
            acc = acc + p_c[(g * GROUP + r) * nq:(g * GROUP + r + 1) * nq]
        ps.append(acc)
    ps = jnp.concatenate(ps, axis=0)
    ps_hi = ps.astype(BF16)
    ps_lo = (ps - ps_hi.astype(F32)).astype(BF16)
    imp = _dot(ps_hi, selmap) + _dot(ps_lo, selmap)
    rows = N_KV * nq
    t = t0 + (lax.broadcasted_iota(jnp.int32, (rows, 1), 0) & (nq - 1))
    jb = lax.broadcasted_iota(jnp.int32, (rows, LANES), 1)
    cur = t >> 6
    valid = (jb * SEL_LEN) <= t
    forced = (jb == 0) | (jb == cur) | (jb == cur - 1)
    score = jnp.where(valid, imp + FORCE_BONUS * forced.astype(F32), -1.0)
    score = jnp.where(jb < n_slc, score, -2.0)
    jbf = jb.astype(F32)
    sel = jnp.zeros((rows, LANES), F32)
    for _ in range(N_SEL):
        m = jnp.max(score, axis=-1, keepdims=True)
        idx = jnp.min(jnp.where(score == m, jbf, 1e9), axis=-1, keepdims=True)
        hit = jbf == idx
        sel = jnp.where(hit, 1.0, sel)
        score = jnp.where(hit, -3.0, score)
    sel = sel.astype(BF16)
    return jnp.concatenate([sel[:nq]] * GROUP + [sel[nq:]] * GROUP, axis=0)


def _combine_heads(gates, o_c, o_s, o_w, nq):
    lane = lax.broadcasted_iota(jnp.int32, (nq, LANES), 1)
    heads = []
    for h in range(N_HEADS):
        sl = slice(h * nq, (h + 1) * nq)
        o = (gates[:, h:h + 1] * o_c[sl]
             + gates[:, N_HEADS + h:N_HEADS + h + 1] * o_s[sl]
             + gates[:, 2 * N_HEADS + h:2 * N_HEADS + h + 1] * o_w[sl])
        if (h % 2) != (h // GROUP):
            o = pltpu.roll(o, HEAD_DIM, 1)
        heads.append(o)
    chunks = [jnp.where(lane < HEAD_DIM, heads[2 * c], heads[2 * c + 1]) for c in range(N_HEADS // 2)]
    return jnp.concatenate(chunks, axis=1)


def _in_proj_kernel(x_ref, g_ref, wq_ref, wg_ref, wu_ref, q_ref, kv_ref, win_ref, gate_ref, u_ref):
    zb = _rms(x_ref[...], g_ref[...]).astype(BF16)
    qkv = _dot(zb, wq_ref[...])
    q_ref[...] = qkv[:, :ATTN_W]
    kv_ref[...] = qkv[:, ATTN_W:ATTN_W + 4 * KV_W]
    win_ref[...] = qkv[:, ATTN_W + 4 * KV_W:]
    gate_ref[...] = _sigmoid(_dot(zb, wg_ref[...]))
    u2 = _dot(zb, wu_ref[...])
    cc = u2.shape[1] // 2
    u_ref[...] = u2[:, :cc] * _sigmoid(u2[:, cc:])


def _in_proj(x, g, wq, wg, wu, tm):
    n, d = x.shape
    cc = wu.shape[1] // 2
    const = lambda i: (0, 0)
    row = lambda i: (i, 0)
    return pl.pallas_call(
        _in_proj_kernel,
        grid=(n // tm,),
        in_specs=[pl.BlockSpec((tm, d), row), pl.BlockSpec((1, d), const),
                  pl.BlockSpec(wq.shape, const), pl.BlockSpec(wg.shape, const), pl.BlockSpec(wu.shape, const)],
        out_specs=[pl.BlockSpec((tm, ATTN_W), row), pl.BlockSpec((tm, 4 * KV_W), row),
                   pl.BlockSpec((tm, 2 * KV_W), row), pl.BlockSpec((tm, GATE_PAD), row),
                   pl.BlockSpec((tm, cc), row)],
        out_shape=[jax.ShapeDtypeStruct((n, ATTN_W), F32), jax.ShapeDtypeStruct((n, 4 * KV_W), F32),
                   jax.ShapeDtypeStruct((n, 2 * KV_W), F32), jax.ShapeDtypeStruct((n, GATE_PAD), F32),
                   jax.ShapeDtypeStruct((n, cc), F32)],
        compiler_params=_cparams("parallel"),
        name="in_proj",
    )(x, g, wq, wg, wu)


def _compress_prompt_kernel(k_ref, v_ref, wk_ref, wv_ref, bk_ref, bv_ref, kc_ref, vc_ref, sh_ref, *, n_rows):
    acc_k = jnp.zeros((n_rows, 2 * KV_W), F32)
    acc_v = jnp.zeros((n_rows, 2 * KV_W), F32)
    for j in range(CMP_STRIDE):
        rk = k_ref[pl.ds(j, n_rows, stride=CMP_STRIDE), :].astype(BF16)
        rv = v_ref[pl.ds(j, n_rows, stride=CMP_STRIDE), :].astype(BF16)
        acc_k = acc_k + _dot(rk, wk_ref[j])
        acc_v = acc_v + _dot(rv, wv_ref[j])
    sh_ref[0:n_rows, 0:KV_W] = acc_k[:, KV_W:]
    sh_ref[0:n_rows, KV_W:] = acc_v[:, KV_W:]
    sh_ref[n_rows:n_rows + SUBLANES, :] = jnp.zeros((SUBLANES, 2 * KV_W), F32)
    nxt = sh_ref[pl.ds(1, n_rows), :]
    kc_ref[...] = acc_k[:, :KV_W] + nxt[:, :KV_W] + bk_ref[...]
    vc_ref[...] = acc_v[:, :KV_W] + nxt[:, KV_W:] + bv_ref[...]


def _compress_prompt(kv, wk, wv, bk, bv, batch, seq):
    n_rows = seq // CMP_STRIDE
    const2 = lambda b: (0, 0)
    const3 = lambda b: (0, 0, 0)
    return pl.pallas_call(
        functools.partial(_compress_prompt_kernel, n_rows=n_rows),
        grid=(batch,),
        in_specs=[pl.BlockSpec((seq, KV_W), lambda b: (b, 0)), pl.BlockSpec((seq, KV_W), lambda b: (b, 1)),
                  pl.BlockSpec(wk.shape, const3), pl.BlockSpec(wv.shape, const3),
                  pl.BlockSpec((1, KV_W), const2), pl.BlockSpec((1, KV_W), const2)],
        out_specs=[pl.BlockSpec((n_rows, KV_W), lambda b: (b, 0))] * 2,
        out_shape=[jax.ShapeDtypeStruct((batch * n_rows, KV_W), F32)] * 2,
        scratch_shapes=[pltpu.VMEM((n_rows + SUBLANES, 2 * KV_W), F32)],
        compiler_params=_cparams("parallel"),
        name="compress_prompt",
    )(kv, kv, wk, wv, bk, bv)


def _attn_prompt_kernel(q_ref, gate_ref, kvs_ref, win_ref, kc_ref, vc_ref, e_ref, selmap_ref, o_ref,
                        *, seq, tk):
    nq = Q_BLOCK
    start = pl.program_id(1) * nq
    q2 = _build_q2(q_ref[...], nq)
    t_rows = _row_positions(nq, start)

    n_cmp = kc_ref.shape[0]
    s_c = _dot_nt(q2, kc_ref[...].astype(BF16))
    cmp_end = lax.broadcasted_iota(jnp.int32, (1, n_cmp), 1) * CMP_STRIDE + (CMP_LEN - 1)
    e_c, inv_c = _softmax_parts(s_c, cmp_end <= t_rows)
    p_c = e_c * inv_c
    o_c = _dot(p_c.astype(BF16), vc_ref[...].astype(BF16))
    sel_rows = _select_blocks(p_c, nq, start, selmap_ref[...], seq // SEL_LEN)

    n_tiles = (start + nq + tk - 1) // tk
    rows = N_HEADS * nq

    def body(j, carry):
        m, l, acc = carry
        k0 = pl.multiple_of(j * tk, tk)
        ks = kvs_ref[pl.ds(k0, tk), 0:KV_W].astype(BF16)
        vs = kvs_ref[pl.ds(k0, tk), KV_W:2 * KV_W].astype(BF16)
        s = _dot_nt(q2, ks)
        mk = _dot(sel_rows, e_ref[j])
        kpos = k0 + lax.broadcasted_iota(jnp.int32, (1, tk), 1)
        mask = (mk > 0.5) & (kpos <= t_rows)
        s = jnp.where(mask, s, NEG)
        m_new = jnp.maximum(m, jnp.max(s, axis=-1, keepdims=True))
        alpha = jnp.exp(m - m_new)
        p = jnp.where(mask, jnp.exp(s - m_new), 0.0)
        l = alpha * l + jnp.sum(p, axis=-1, keepdims=True)
        acc = alpha * acc + _dot(p.astype(BF16), vs)
        return m_new, l, acc

    m0 = jnp.full((rows, 1), NEG, F32)
    l0 = jnp.zeros((rows, 1), F32)
    a0 = jnp.zeros((rows, KV_W), F32)
    _, l_s, acc_s = lax.fori_loop(0, n_tiles, body, (m0, l0, a0))
    o_s = acc_s * (1.0 / jnp.maximum(l_s, TINY))

    span = WINDOW + nq
    ws = pl.multiple_of(jnp.maximum(start - WINDOW, 0), nq)
    kw = win_ref[pl.ds(ws, span), 0:KV_W].astype(BF16)
    vw = win_ref[pl.ds(ws, span), KV_W:2 * KV_W].astype(BF16)
    s_w = _dot_nt(q2, kw)
    dt = t_rows - (ws + lax.broadcasted_iota(jnp.int32, (1, span), 1))
    e_w, inv_w = _softmax_parts(s_w, (dt >= 0) & (dt <= WINDOW))
    o_w = _dot(e_w.astype(BF16), vw) * inv_w

    o_ref[...] = _combine_heads(gate_ref[...], o_c, o_s, o_w, nq)


def _attn_prompt(q, gates, kv, win, kcmp, vcmp, e_mat, selmap, batch, seq, tk):
    nb = seq // Q_BLOCK
    n_cmp = kcmp.shape[0] // batch
    blk = lambda b, i: (b * nb + i, 0)
    per_b = lambda b, i: (b, 0)
    const = lambda b, i: (0, 0)
    const3 = lambda b, i: (0, 0, 0)
    return pl.pallas_call(
        functools.partial(_attn_prompt_kernel, seq=seq, tk=tk),
        grid=(batch, nb),
        in_specs=[pl.BlockSpec((Q_BLOCK, ATTN_W), blk), pl.BlockSpec((Q_BLOCK, GATE_PAD), blk),
                  pl.BlockSpec((seq, 2 * KV_W), lambda b, i: (b, 1)),
                  pl.BlockSpec((seq, 2 * KV_W), per_b),
                  pl.BlockSpec((n_cmp, KV_W), per_b), pl.BlockSpec((n_cmp, KV_W), per_b),
                  pl.BlockSpec(e_mat.shape, const3), pl.BlockSpec(selmap.shape, const)],
        out_specs=pl.BlockSpec((Q_BLOCK, ATTN_W), blk),
        out_shape=jax.ShapeDtypeStruct((batch * seq, ATTN_W), F32),
        compiler_params=_cparams("parallel", "parallel"),
        name="attn_prompt",
    )(q, gates, kv, win, kcmp, vcmp, e_mat, selmap)


def _attn_sample_kernel(pt_ref, q_ref, gate_ref, kvn_ref, winn_ref, wbuf_ref, *rest, n_pages, past):
    pages = rest[:n_pages]
    (wk_ref, wv_ref, bk_ref, bv_ref, e_ref, selmap_ref, o_ref, sh_ref, kraw_ref, vraw_ref) = rest[n_pages:]
    del pt_ref
    nq = q_ref.shape[0]
    page = pages[0].shape[0]
    n_rows = past // CMP_STRIDE
    q2 = _build_q2(q_ref[...], nq)
    t_rows = _row_positions(nq, past)

    for p, pg in enumerate(pages):
        kraw_ref[p * page:(p + 1) * page, :] = pg[:, 0:KV_W]
        vraw_ref[p * page:(p + 1) * page, :] = pg[:, KV_W:2 * KV_W]
    acc_k = jnp.zeros((n_rows, 2 * KV_W), F32)
    acc_v = jnp.zeros((n_rows, 2 * KV_W), F32)
    for j in range(CMP_STRIDE):
        rk = kraw_ref[pl.ds(j, n_rows, stride=CMP_STRIDE), :]
        rv = vraw_ref[pl.ds(j, n_rows, stride=CMP_STRIDE), :]
        acc_k = acc_k + _dot(rk.astype(BF16), wk_ref[j])
        acc_v = acc_v + _dot(rv.astype(BF16), wv_ref[j])
    sh_ref[0:n_rows, 0:KV_W] = acc_k[:, KV_W:]
    sh_ref[0:n_rows, KV_W:] = acc_v[:, KV_W:]
    sh_ref[n_rows:n_rows + SUBLANES, :] = jnp.zeros((SUBLANES, 2 * KV_W), F32)
    nxt = sh_ref[pl.ds(1, n_rows), :]
    kcmp = (acc_k[:, :KV_W] + nxt[:, :KV_W] + bk_ref[...]).astype(BF16)
    vcmp = (acc_v[:, :KV_W] + nxt[:, KV_W:] + bv_ref[...]).astype(BF16)

    s_c = _dot_nt(q2, kcmp)
    cmp_end = lax.broadcasted_iota(jnp.int32, (1, n_rows), 1) * CMP_STRIDE + (CMP_LEN - 1)
    n_cmp = (past + nq - CMP_LEN) // CMP_STRIDE + 1
    n_id = lax.broadcasted_iota(jnp.int32, (1, n_rows), 1)
    e_c, inv_c = _softmax_parts(s_c, (cmp_end <= t_rows) & (n_id < n_cmp))
    p_c = e_c * inv_c
    o_c = _dot(p_c.astype(BF16), vcmp)
    n_slc = -(-(past + nq) // SEL_LEN)
    sel_rows = _select_blocks(p_c, nq, past, selmap_ref[...], n_slc)

    zpad = jnp.zeros((page - nq, KV_W), F32)
    k_new = jnp.concatenate([kvn_ref[:, 2 * KV_W:3 * KV_W], zpad], axis=0).astype(BF16)
    v_new = jnp.concatenate([kvn_ref[:, 3 * KV_W:4 * KV_W], zpad], axis=0).astype(BF16)
    s_parts = [_dot_nt(q2, pg[:, 2 * KV_W:3 * KV_W].astype(BF16)) for pg in pages]
    s_parts.append(_dot_nt(q2, k_new))
    s_s = jnp.concatenate(s_parts, axis=1)
    n_keys = (n_pages + 1) * page
    mk = _dot(sel_rows, e_ref[0])
    kpos = lax.broadcasted_iota(jnp.int32, (1, n_keys), 1)
    e_s, inv_s = _softmax_parts(s_s, (mk > 0.5) & (kpos <= t_rows))
    e_s = e_s.astype(BF16)
    acc = _dot(e_s[:, n_pages * page:], v_new)
    for p, pg in enumerate(pages):
        acc = acc + _dot(e_s[:, p * page:(p + 1) * page], pg[:, 3 * KV_W:4 * KV_W].astype(BF16))
    o_s = acc * inv_s

    wb = wbuf_ref.shape[0]
    k_w = jnp.concatenate([wbuf_ref[:, 0:KV_W], winn_ref[:, 0:KV_W], zpad], axis=0).astype(BF16)
    v_w = jnp.concatenate([wbuf_ref[:, KV_W:], winn_ref[:, KV_W:], zpad], axis=0).astype(BF16)
    s_w = _dot_nt(q2, k_w)
    dt = t_rows - (past - wb + lax.broadcasted_iota(jnp.int32, (1, wb + page), 1))
    e_w, inv_w = _softmax_parts(s_w, (dt >= 0) & (dt <= WINDOW))
    o_w = _dot(e_w.astype(BF16), v_w) * inv_w

    o_ref[...] = _combine_heads(gate_ref[...], o_c, o_s, o_w, nq)


def _attn_sample(page_table, q, gates, kvn, winn, wbuf, cache, layer, wk, wv, bk, bv, e_mat, selmap):
    db, nq, _ = q.shape
    n_pages = page_table.shape[1]
    page = cache.shape[2]
    past = n_pages * page
    wb = wbuf.shape[2]
    seq3 = lambda s, pt: (s, 0, 0)
    const2 = lambda s, pt: (0, 0)
    const3 = lambda s, pt: (0, 0, 0)

    def page_spec(p):
        return pl.BlockSpec((None, None, page, 4 * KV_W), lambda s, pt: (layer, pt[s, p], 0, 0))

    grid_spec = pltpu.PrefetchScalarGridSpec(
        num_scalar_prefetch=1,
        grid=(db,),
        in_specs=[pl.BlockSpec((None, nq, ATTN_W), seq3), pl.BlockSpec((None, nq, GATE_PAD), seq3),
                  pl.BlockSpec((None, nq, 4 * KV_W), seq3), pl.BlockSpec((None, nq, 2 * KV_W), seq3),
                  pl.BlockSpec((None, None, wb, 2 * KV_W), lambda s, pt: (layer, s, 0, 0))]
                 + [page_spec(p) for p in range(n_pages)]
                 + [pl.BlockSpec(wk.shape, const3), pl.BlockSpec(wv.shape, const3),
                    pl.BlockSpec((1, KV_W), const2), pl.BlockSpec((1, KV_W), const2),
                    pl.BlockSpec(e_mat.shape, const3), pl.BlockSpec(selmap.shape, const2)],
        out_specs=pl.BlockSpec((None, nq, ATTN_W), seq3),
        scratch_shapes=[pltpu.VMEM((past // CMP_STRIDE + SUBLANES, 2 * KV_W), F32),
                        pltpu.VMEM((past, KV_W), F32), pltpu.VMEM((past, KV_W), F32)],
    )
    return pl.pallas_call(
        functools.partial(_attn_sample_kernel, n_pages=n_pages, past=past),
        grid_spec=grid_spec,
        out_shape=jax.ShapeDtypeStruct((db, nq, ATTN_W), F32),
        compiler_params=_cparams("parallel"),
        name="attn_sample",
    )(page_table, q, gates, kvn, winn, wbuf, *([cache] * n_pages), wk, wv, bk, bv, e_mat, selmap)


def _ln_silu(y, g, b):
    mu = jnp.mean(y, axis=-1, keepdims=True)
    yc = y - mu
    var = jnp.mean(yc * yc, axis=-1, keepdims=True)
    z = yc * lax.rsqrt(var + EPS) * g + b
    return z * _sigmoid(z)


CONV_HALO = 32
CONV_ROWS = 64


def _conv_prompt_kernel(u_ref, prev_ref, w_ref, b_ref, lg_ref, lb_ref, o_ref, ext_ref, *, tiles_per_seq):
    tm = u_ref.shape[0]
    first = (pl.program_id(0) % tiles_per_seq) == 0
    ext_ref[0:CONV_HALO, :] = jnp.where(first, 0.0, prev_ref[...])
    ext_ref[CONV_HALO:CONV_HALO + tm, :] = u_ref[...]
    off = CONV_HALO - (CONV_WIDTH - 1)

    for r0 in range(0, tm, CONV_ROWS):
        acc = jnp.zeros((CONV_ROWS, u_ref.shape[1]), F32) + b_ref[...]
        for k in range(CONV_WIDTH):
            acc = acc + w_ref[k:k + 1, :] * ext_ref[r0 + off + k:r0 + off + k + CONV_ROWS, :]
        o_ref[r0:r0 + CONV_ROWS, :] = _ln_silu(acc, lg_ref[...], lb_ref[...])


def _conv_prompt(u, w, b, lg, lb, seq, tm):
    n, cc = u.shape
    const = lambda i: (0, 0)
    per_halo = tm // CONV_HALO
    return pl.pallas_call(
        functools.partial(_conv_prompt_kernel, tiles_per_seq=seq // tm),
        grid=(n // tm,),
        in_specs=[pl.BlockSpec((tm, cc), lambda i: (i, 0)),
                  pl.BlockSpec((CONV_HALO, cc), lambda i: (jnp.maximum(i * per_halo - 1, 0), 0)),
                  pl.BlockSpec(w.shape, const), pl.BlockSpec((1, cc), const),
                  pl.BlockSpec((1, cc), const), pl.BlockSpec((1, cc), const)],
        out_specs=pl.BlockSpec((tm, cc), lambda i: (i, 0)),
        out_shape=jax.ShapeDtypeStruct((n, cc), F32),
        scratch_shapes=[pltpu.VMEM((CONV_HALO + tm, cc), F32)],
        compiler_params=_cparams("parallel"),
        name="conv_prompt",
    )(u, u, w, b, lg, lb)


def _conv_sample_kernel(ext_ref, w_ref, b_ref, lg_ref, lb_ref, o_ref):
    ns, nq, cc = o_ref.shape
    for s in range(ns):
        acc = jnp.zeros((nq, cc), F32) + b_ref[...]
        for k in range(CONV_WIDTH):
            acc = acc + w_ref[k:k + 1, :] * ext_ref[s, pl.ds(k, nq), :]
        o_ref[s] = _ln_silu(acc, lg_ref[...], lb_ref[...])


def _conv_sample(ext, w, b, lg, lb, nq, ns):
    db, rows, cc = ext.shape
    const = lambda i: (0, 0)
    return pl.pallas_call(
        _conv_sample_kernel,
        grid=(db // ns,),
        in_specs=[pl.BlockSpec((ns, rows, cc), lambda i: (i, 0, 0)),
                  pl.BlockSpec(w.shape, const), pl.BlockSpec((1, cc), const),
                  pl.BlockSpec((1, cc), const), pl.BlockSpec((1, cc), const)],
        out_specs=pl.BlockSpec((ns, nq, cc), lambda i: (i, 0, 0)),
        out_shape=jax.ShapeDtypeStruct((db, nq, cc), F32),
        compiler_params=_cparams("parallel"),
        name="conv_sample",
    )(ext, w, b, lg, lb)


def _merge_kernel(x_ref, gpre_ref, oa_ref, oc_ref, wm_ref, wao_ref, wco_ref, wout_ref, gpost_ref, xo_ref):
    x = x_ref[...]
    d = x.shape[1]
    zb = _rms(x, gpre_ref[...]).astype(BF16)
    gm = _sigmoid(_dot(zb, wm_ref[...]))
    a = _dot(oa_ref[...].astype(BF16), wao_ref[...])
    c = _dot(oc_ref[...].astype(BF16), wco_ref[...])
    mix = gm[:, :d] * a + gm[:, d:] * c
    h = _dot(mix.astype(BF16), wout_ref[...])
    xo_ref[...] = x + _rms(h, gpost_ref[...])


def _merge(x, gpre, oa, oc, wm, wao, wco, wout, gpost, tm):
    n, d = x.shape
    const = lambda i: (0, 0)
    row = lambda i: (i, 0)
    return pl.pallas_call(
        _merge_kernel,
        grid=(n // tm,),
        in_specs=[pl.BlockSpec((tm, d), row), pl.BlockSpec((1, d), const),
                  pl.BlockSpec((tm, oa.shape[1]), row), pl.BlockSpec((tm, oc.shape[1]), row),
                  pl.BlockSpec(wm.shape, const), pl.BlockSpec(wao.shape, const),
                  pl.BlockSpec(wco.shape, const), pl.BlockSpec(wout.shape, const),
                  pl.BlockSpec((1, d), const)],
        out_specs=pl.BlockSpec((tm, d), row),
        out_shape=jax.ShapeDtypeStruct((n, d), F32),
        compiler_params=_cparams("parallel"),
        name="merge",
    )(x, gpre, oa, oc, wm, wao, wco, wout, gpost)


FFN_HALO = SUBLANES


def _ffn_kernel(*refs, sample, tiles_per_seq, tf, nq):
    if sample:
        (x_ref, gpre_ref, wg_ref, wv_ref, cw_ref, cb_ref, wd_ref, gpost_ref, h1_ref, h2_ref,
         xo_ref, up_ref, ext_ref) = refs
    else:
        (x_ref, gpre_ref, wg_ref, wv_ref, cw_ref, cb_ref, wd_ref, gpost_ref,
         xo_ref, up_ref, ext_ref, carry_ref) = refs
    x = x_ref[...]
    tm, d = x.shape
    dff = wd_ref.shape[0]
    zb = _rms(x, gpre_ref[...]).astype(BF16)
    first = (pl.program_id(0) % tiles_per_seq) == 0
    tpos = lax.broadcasted_iota(jnp.int32, (tm, 1), 0) & (nq - 1)
    if not sample:
        @pl.when(pl.program_id(0) == 0)
        def _():
            carry_ref[...] = jnp.zeros(carry_ref.shape, F32)

    def conv_half(w_ref, c0, col0):
        up = _dot(zb, w_ref[:, c0:c0 + tf])
        ext_ref[FFN_HALO:FFN_HALO + tm, :] = up
        if sample:
            ext_ref[0:FFN_HALO, :] = jnp.zeros((FFN_HALO, tf), F32)
            up_ref[:, col0:col0 + tf] = up
        else:
            ext_ref[0:FFN_HALO, :] = jnp.where(first, 0.0, carry_ref[:, col0:col0 + tf])
            carry_ref[:, col0:col0 + tf] = up[tm - FFN_HALO:, :]
            up_ref[:, col0:col0 + tf] = up[tm - FFN_HALO:, :]
        s1 = ext_ref[pl.ds(FFN_HALO - 1, tm), :]
        s2 = ext_ref[pl.ds(FFN_HALO - 2, tm), :]
        if sample:
            s1 = jnp.where(tpos == 0, h1_ref[:, col0:col0 + tf], s1)
            s2 = jnp.where(tpos <= 1, h2_ref[:, col0:col0 + tf], s2)
        cw = cw_ref[:, col0:col0 + tf]
        return cw[0:1] * s2 + cw[1:2] * s1 + cw[2:3] * up + cb_ref[:, col0:col0 + tf]

    acc = jnp.zeros((tm, d), F32)
    for c in range(dff // tf):
        hg = conv_half(wg_ref, c * tf, c * tf)
        hv = conv_half(wv_ref, c * tf, dff + c * tf)
        act = (_gelu_tanh(hg) * hv).astype(BF16)
        acc = acc + _dot(act, wd_ref[c * tf:(c + 1) * tf, :])
    xo_ref[...] = x + _rms(acc, gpost_ref[...])


def _ffn(x, gpre, wg, wv, cw, cb, wd, gpost, tm, tf, seq=None, hist=None, nq=None):
    n, d = x.shape
    dff = wd.shape[0]
    sample = hist is not None
    const = lambda i: (0, 0)
    row = lambda i: (i, 0)
    single = pl.Buffered(1)
    in_specs = [pl.BlockSpec((tm, d), row), pl.BlockSpec((1, d), const),
                pl.BlockSpec(wg.shape, const, pipeline_mode=single),
                pl.BlockSpec(wv.shape, const, pipeline_mode=single),
                pl.BlockSpec(cw.shape, const), pl.BlockSpec(cb.shape, const),
                pl.BlockSpec(wd.shape, const, pipeline_mode=single), pl.BlockSpec((1, d), const)]
    args = [x, gpre, wg, wv, cw, cb, wd, gpost]
    scratch = [pltpu.VMEM((FFN_HALO + tm, tf), F32)]
    if sample:
        in_specs += [pl.BlockSpec((tm, 2 * dff), row)] * 2
        args += list(hist)
        up_rows, tiles_per_seq = tm, 1
    else:
        scratch.append(pltpu.VMEM((FFN_HALO, 2 * dff), F32))
        up_rows, tiles_per_seq, nq = FFN_HALO, seq // tm, tm
    return pl.pallas_call(
        functools.partial(_ffn_kernel, sample=sample, tiles_per_seq=tiles_per_seq, tf=tf, nq=nq),
        grid=(n // tm,),
        in_specs=in_specs,
        out_specs=[pl.BlockSpec((tm, d), row), pl.BlockSpec((up_rows, 2 * dff), row)],
        out_shape=[jax.ShapeDtypeStruct((n, d), F32),
                   jax.ShapeDtypeStruct((n // tm * up_rows, 2 * dff), F32)],
        scratch_shapes=scratch,
        compiler_params=_cparams("arbitrary"),
        name="ffn_sample" if sample else "ffn_prompt",
    )(*args)


def _block_expand_table(n_keys, tk):
    k = np.arange(n_keys)[None, :] // SEL_LEN
    m = (np.arange(LANES)[:, None] == k).astype(np.float32)
    m = m.reshape(LANES, n_keys // tk, tk).transpose(1, 0, 2)
    return jnp.asarray(m, dtype=BF16)


def _cmp_sel_table(n_rows, n_cmp, n_slc):
    s = np.arange(n_rows) * CMP_STRIDE
    e = s + CMP_LEN - 1
    js = np.arange(LANES) * SEL_LEN
    je = js + SEL_LEN - 1
    m = (s[:, None] <= je[None, :]) & (e[:, None] >= js[None, :])
    m &= (np.arange(n_rows)[:, None] < n_cmp) & (np.arange(LANES)[None, :] < n_slc)
    return jnp.asarray(m.astype(np.float32), dtype=BF16)


def _phi_weights(w):
    depth = w.shape[0]
    wj = w.reshape(depth, CMP_LEN, HEAD_DIM, HEAD_DIM)
    eye = jnp.eye(N_KV, dtype=w.dtype)
    bd = jnp.einsum("gh,ljde->ljgdhe", eye, wj).reshape(depth, CMP_LEN, KV_W, KV_W)
    return jnp.concatenate([bd[:, :CMP_STRIDE], bd[:, CMP_STRIDE:]], axis=-1).astype(BF16)


def kernel(x_prompt, x_sample, cache_kv, state_kv_win, state_conv, state_ffn, page_table, norm_mix_pre,
           norm_mix_post, norm_ffn_pre, norm_ffn_post, w_in, w_phi_k, b_phi_k, w_phi_v, b_phi_v, w_attn_out,
           conv_w, conv_b, conv_ln_g, conv_ln_b, w_conv_out, w_out, w_up, ffn_conv_w, ffn_conv_b, w_down):
    batch, seq, d = x_prompt.shape
    db, nq, _ = x_sample.shape
    depth, n_phys, page = cache_kv.shape[:3]
    n_pages = page_table.shape[1]
    past = n_pages * page
    wb = state_kv_win.shape[2]
    cc = conv_w.shape[2]
    dff = w_down.shape[1]
    assert seq % Q_BLOCK == 0 and nq == SUBLANES and page == LANES and wb == WINDOW

    tm = 512
    tm_s = 128
    tf = 256
    tk = 512
    assert dff % tf == 0

    c1 = ATTN_W + 6 * KV_W
    c2 = c1 + 3 * N_HEADS
    c3 = c2 + 2 * cc
    wq = w_in[:, :, :c1].astype(BF16)
    wgate = jnp.pad(w_in[:, :, c1:c2], ((0, 0), (0, 0), (0, GATE_PAD - 3 * N_HEADS))).astype(BF16)
    wu = w_in[:, :, c2:c3].astype(BF16)
    wm = w_in[:, :, c3:].astype(BF16)
    wao = w_attn_out.astype(BF16)
    wco = w_conv_out.astype(BF16)
    wout = w_out.astype(BF16)
    wup_g = w_up[:, :, :dff].astype(BF16)
    wup_v = w_up[:, :, dff:].astype(BF16)
    wdn = w_down.astype(BF16)
    wk_ab = _phi_weights(w_phi_k)
    wv_ab = _phi_weights(w_phi_v)
    bk2 = jnp.tile(b_phi_k, (1, N_KV)).reshape(depth, 1, KV_W)
    bv2 = jnp.tile(b_phi_v, (1, N_KV)).reshape(depth, 1, KV_W)

    n_cmp_p = (seq - CMP_LEN) // CMP_STRIDE + 1
    e_p = _block_expand_table(seq, tk)
    selmap_p = _cmp_sel_table(seq // CMP_STRIDE, n_cmp_p, seq // SEL_LEN)
    n_cmp_s = (past + nq - CMP_LEN) // CMP_STRIDE + 1
    e_s = _block_expand_table(past + page, past + page)
    selmap_s = _cmp_sel_table(past // CMP_STRIDE, n_cmp_s, -(-(past + nq) // SEL_LEN))

    cache = cache_kv.reshape(depth, n_phys, page, 4 * KV_W)
    wbuf = state_kv_win.reshape(depth, db, wb, 2 * KV_W)
    row = lambda v: v.reshape(1, -1)

    xp = x_prompt.reshape(batch * seq, d)
    xs = x_sample.reshape(db * nq, d)
    outs = [[] for _ in range(8)]
    for l in range(depth):
        gpre, gpost = row(norm_mix_pre[l]), row(norm_mix_post[l])
        fpre, fpost = row(norm_ffn_pre[l]), row(norm_ffn_post[l])
        cb, lg, lb = row(conv_b[l]), row(conv_ln_g[l]), row(conv_ln_b[l])
        fcb = row(ffn_conv_b[l])

        q, kv, win, gates, u = _in_proj(xp, gpre, wq[l], wgate[l], wu[l], tm)
        kcmp, vcmp = _compress_prompt(kv, wk_ab[l], wv_ab[l], bk2[l], bv2[l], batch, seq)
        oa = _attn_prompt(q, gates, kv, win, kcmp, vcmp, e_p, selmap_p, batch, seq, tk)
        oc = _conv_prompt(u, conv_w[l], cb, lg, lb, seq, tm)
        xp = _merge(xp, gpre, oa, oc, wm[l], wao[l], wco[l], wout[l], gpost, tm)
        xp, up_tail = _ffn(xp, fpre, wup_g[l], wup_v[l], ffn_conv_w[l], fcb, wdn[l], fpost, tm, tf, seq=seq)
        outs[0].append(kv.reshape(batch, seq, 4, N_KV, HEAD_DIM))
        outs[2].append(win.reshape(batch, seq, 2, N_KV, HEAD_DIM)[:, seq - min(WINDOW, seq):])
        outs[4].append(u.reshape(batch, seq, cc)[:, seq - (CONV_WIDTH - 1):])
        outs[6].append(up_tail.reshape(batch, seq // tm, FFN_HALO, 2 * dff)[:, -1, FFN_HALO - (FFN_CONV_WIDTH - 1):])

        q, kv, win, gates, u = _in_proj(xs, gpre, wq[l], wgate[l], wu[l], tm)
        oa = _attn_sample(page_table, q.reshape(db, nq, ATTN_W), gates.reshape(db, nq, GATE_PAD),
                          kv.reshape(db, nq, 4 * KV_W), win.reshape(db, nq, 2 * KV_W), wbuf, cache, l,
                          wk_ab[l], wv_ab[l], bk2[l], bv2[l], e_s, selmap_s)
        ext = jnp.concatenate([state_conv[l], u.reshape(db, nq, cc)], axis=1)
        oc = _conv_sample(ext, conv_w[l], cb, lg, lb, nq, 16)
        xs = _merge(xs, gpre, oa.reshape(db * nq, ATTN_W), oc.reshape(db * nq, cc),
                    wm[l], wao[l], wco[l], wout[l], gpost, tm)
        hist = state_ffn[l]
        h2 = jnp.pad(hist, ((0, 0), (0, nq - (FFN_CONV_WIDTH - 1)), (0, 0))).reshape(db * nq, 2 * dff)
        h1 = jnp.pad(hist[:, 1:], ((0, 0), (0, nq - 1), (0, 0))).reshape(db * nq, 2 * dff)
        xs, up_s = _ffn(xs, fpre, wup_g[l], wup_v[l], ffn_conv_w[l], fcb, wdn[l], fpost, tm_s, tf,
                        hist=(h1, h2), nq=nq)
        outs[1].append(kv.reshape(db, nq, 4, N_KV, HEAD_DIM))
        win_all = jnp.concatenate([state_kv_win[l], win.reshape(db, nq, 2, N_KV, HEAD_DIM)], axis=1)
        outs[3].append(win_all[:, win_all.shape[1] - WINDOW:])
        outs[5].append(ext[:, nq:])
        outs[7].append(up_s.reshape(db, nq, 2 * dff)[:, nq - (FFN_CONV_WIDTH - 1):])

    return (xp.reshape(batch, seq, d), xs.reshape(db, nq, d),
            jnp.stack(outs[0]), jnp.stack(outs[1]), jnp.stack(outs[2]), jnp.stack(outs[3]),
            jnp.stack(outs[4]), jnp.stack(outs[5]), jnp.stack(outs[6]), jnp.stack(outs[7]))
```

```python
import functools

import numpy as np
import jax
import jax.numpy as jnp
from jax import lax
from jax.experimental import pallas as pl
from jax.experimental.pallas import tpu as pltpu

F32 = jnp.float32
BF16 = jnp.bfloat16

N_HEADS = 8
N_KV = 2
GROUP = N_HEADS // N_KV
HEAD_DIM = 64
ATTN_W = N_HEADS * HEAD_DIM
KV_W = N_KV * HEAD_DIM
CMP_LEN = 32
CMP_STRIDE = 16
SEL_LEN = 64
N_SEL = 8
WINDOW = 512
Q_BLOCK = 128
FORCE_BONUS = 1000.0
CONV_WIDTH = 31
FFN_CONV_WIDTH = 3
EPS = 1e-6
TINY = 1e-30
NEG_BIG = -(2.0 ** 100)
LOG2E = 1.4426950408889634
GATE_PAD = 128

LANES = 128
SUBLANES = 8
VMEM_LIMIT = 56 * 1024 * 1024


def _cparams(*sem):
    return pltpu.CompilerParams(dimension_semantics=sem, vmem_limit_bytes=VMEM_LIMIT)


def _rms(x, g):
    return x * lax.rsqrt(jnp.mean(x * x, axis=-1, keepdims=True) + EPS) * g


def _sigmoid(x):
    return 1.0 / (1.0 + jnp.exp(-x))


def _gelu_tanh(x):
    c = np.sqrt(2.0 / np.pi).astype(np.float32)
    return x * (0.5 * (1.0 + jnp.tanh(c * (x + 0.044715 * (x * x * x)))))


def _dot(a, b):
    return jnp.dot(a, b, preferred_element_type=F32)


def _dot_nt(a, b):
    return lax.dot_general(a, b, (((1,), (1,)), ((), ())), preferred_element_type=F32)


def _exp2_rows(s):
    m = jnp.max(s, axis=-1, keepdims=True)
    e = jnp.exp2(s - m)
    return e, jnp.sum(e, axis=-1, keepdims=True)


def _build_q2(q, nq):
    lane = lax.broadcasted_iota(jnp.int32, (nq, LANES), 1)
    parts = []
    for h in range(N_HEADS):
        g = h // GROUP
        c = h // 2
        chunk = q[:, c * LANES:(c + 1) * LANES]
        if (h % 2) != g:
            chunk = pltpu.roll(chunk, HEAD_DIM, 1)
        keep = (lane >= g * HEAD_DIM) & (lane < (g + 1) * HEAD_DIM)
        parts.append(jnp.where(keep, chunk * (HEAD_DIM ** -0.5 * LOG2E), 0.0))
    return jnp.concatenate(parts, axis=0).astype(BF16)


def _row_positions(nq, t0):
    r = lax.broadcasted_iota(jnp.int32, (N_HEADS * nq, 1), 0)
    return t0 + (r & (nq - 1))


def _unselected_blocks_t(imp_t, t_lane, n_slc):
    jb = lax.broadcasted_iota(jnp.int32, imp_t.shape, 0)
    cur = t_lane >> 6
    valid = (jb * SEL_LEN) <= t_lane
    forced = (jb == 0) | (jb == cur) | (jb == cur - 1)
    score = jnp.where(valid, imp_t + FORCE_BONUS * forced.astype(F32), -1.0)
    score = jnp.where(jb < n_slc, score, -2.0)
    jbf = jb.astype(F32)
    unsel = jnp.ones(imp_t.shape, F32)
    for _ in range(N_SEL):
        m = jnp.max(score, axis=0, keepdims=True)
        idx = jnp.min(jnp.where(score == m, jbf, 1e9), axis=0, keepdims=True)
        hit = jbf == idx
        unsel = jnp.where(hit, 0.0, unsel)
        score = jnp.where(hit, -3.0, score)
    return unsel


def _split_dot(w, x):
    hi = x.astype(BF16)
    lo = (x - hi.astype(F32)).astype(BF16)
    return _dot(w, hi) + _dot(w, lo)


def _combine_heads(gates, o_c, o_s, o_w, nq):
    lane = lax.broadcasted_iota(jnp.int32, (nq, LANES), 1)
    heads = []
    for h in range(N_HEADS):
        sl = slice(h * nq, (h + 1) * nq)
        o = (gates[:, h:h + 1] * o_c[sl]
             + gates[:, N_HEADS + h:N_HEADS + h + 1] * o_s[sl]
             + gates[:, 2 * N_HEADS + h:2 * N_HEADS + h + 1] * o_w[sl])
        if (h % 2) != (h // GROUP):
            o = pltpu.roll(o, HEAD_DIM, 1)
        heads.append(o)
    chunks = [jnp.where(lane < HEAD_DIM, heads[2 * c], heads[2 * c + 1]) for c in range(N_HEADS // 2)]
    return jnp.concatenate(chunks, axis=1)


def _in_proj_kernel(x_ref, g_ref, wq_ref, wg_ref, wu_ref, q_ref, kv_ref, win_ref, att_ref, gate_ref, u_ref):
    zb = _rms(x_ref[...], g_ref[...]).astype(BF16)
    qkv = _dot(zb, wq_ref[...])
    q_ref[...] = qkv[:, :ATTN_W]
    kv_ref[...] = qkv[:, ATTN_W:ATTN_W + 4 * KV_W]
    win_ref[...] = qkv[:, ATTN_W + 4 * KV_W:]
    att_ref[...] = qkv[:, ATTN_W + 2 * KV_W:].astype(BF16)
    gate_ref[...] = _sigmoid(_dot(zb, wg_ref[...]))
    u2 = _dot(zb, wu_ref[...])
    cc = u2.shape[1] // 2
    u_ref[...] = u2[:, :cc] * _sigmoid(u2[:, cc:])


def _in_proj(x, g, wq, wg, wu, tm):
    n, d = x.shape
    cc = wu.shape[1] // 2
    const = lambda i: (0, 0)
    row = lambda i: (i, 0)
    return pl.pallas_call(
        _in_proj_kernel,
        grid=(n // tm,),
        in_specs=[pl.BlockSpec((tm, d), row), pl.BlockSpec((1, d), const),
                  pl.BlockSpec(wq.shape, const), pl.BlockSpec(wg.shape, const), pl.BlockSpec(wu.shape, const)],
        out_specs=[pl.BlockSpec((tm, ATTN_W), row), pl.BlockSpec((tm, 4 * KV_W), row),
                   pl.BlockSpec((tm, 2 * KV_W), row), pl.BlockSpec((tm, 4 * KV_W), row),
                   pl.BlockSpec((tm, GATE_PAD), row),
                   pl.BlockSpec((tm, cc), row)],
        out_shape=[jax.ShapeDtypeStruct((n, ATTN_W), F32), jax.ShapeDtypeStruct((n, 4 * KV_W), F32),
                   jax.ShapeDtypeStruct((n, 2 * KV_W), F32), jax.ShapeDtypeStruct((n, 4 * KV_W), BF16),
                   jax.ShapeDtypeStruct((n, GATE_PAD), F32),
                   jax.ShapeDtypeStruct((n, cc), F32)],
        compiler_params=_cparams("parallel"),
        name="in_proj",
    )(x, g, wq, wg, wu)


def _compress_prompt_kernel(k_ref, v_ref, wk_ref, wv_ref, bk_ref, bv_ref, kc_ref, vc_ref, sh_ref, *, n_rows):
    acc_k = jnp.zeros((n_rows, 2 * KV_W), F32)
    acc_v = jnp.zeros((n_rows, 2 * KV_W), F32)
    for j in range(CMP_STRIDE):
        rk = k_ref[pl.ds(j, n_rows, stride=CMP_STRIDE), :].astype(BF16)
        rv = v_ref[pl.ds(j, n_rows, stride=CMP_STRIDE), :].astype(BF16)
        acc_k = acc_k + _dot(rk, wk_ref[j])
        acc_v = acc_v + _dot(rv, wv_ref[j])
    sh_ref[0:n_rows, 0:KV_W] = acc_k[:, KV_W:]
    sh_ref[0:n_rows, KV_W:] = acc_v[:, KV_W:]
    sh_ref[n_rows:n_rows + SUBLANES, :] = jnp.zeros((SUBLANES, 2 * KV_W), F32)
    nxt = sh_ref[pl.ds(1, n_rows), :]
    kc_ref[...] = acc_k[:, :KV_W] + nxt[:, :KV_W] + bk_ref[...]
    vc_ref[...] = acc_v[:, :KV_W] + nxt[:, KV_W:] + bv_ref[...]


def _compress_prompt(kv, wk, wv, bk, bv, batch, seq):
    n_rows = seq // CMP_STRIDE
    const2 = lambda b: (0, 0)
    const3 = lambda b: (0, 0, 0)
    return pl.pallas_call(
        functools.partial(_compress_prompt_kernel, n_rows=n_rows),
        grid=(batch,),
        in_specs=[pl.BlockSpec((seq, KV_W), lambda b: (b, 0)), pl.BlockSpec((seq, KV_W), lambda b: (b, 1)),
                  pl.BlockSpec(wk.shape, const3), pl.BlockSpec(wv.shape, const3),
                  pl.BlockSpec((1, KV_W), const2), pl.BlockSpec((1, KV_W), const2)],
        out_specs=[pl.BlockSpec((n_rows, KV_W), lambda b: (b, 0))] * 2,
        out_shape=[jax.ShapeDtypeStruct((batch * n_rows, KV_W), F32)] * 2,
        scratch_shapes=[pltpu.VMEM((n_rows + SUBLANES, 2 * KV_W), F32)],
        compiler_params=_cparams("parallel"),
        name="compress_prompt",
    )(kv, kv, wk, wv, bk, bv)


def _attn_prompt_kernel(q_ref, gate_ref, ksv_ref, kwv_ref, kc_ref, vc_ref, cmpt_ref, selt_ref, cast_ref,
                        wint_ref, onehot_ref, selmap_ref, o_ref, *, seq, tk):
    nq = Q_BLOCK
    rows = N_HEADS * nq
    iblk = pl.program_id(1)
    start = iblk * nq
    q2 = _build_q2(q_ref[...], nq)
    onehot = onehot_ref[...]
    qa = jnp.concatenate([q2, onehot], axis=1)
    t_rows = _row_positions(nq, start)

    kca = jnp.concatenate([kc_ref[...].astype(BF16), cmpt_ref[...]], axis=1)
    e_c, l_c = _exp2_rows(_dot_nt(qa, kca))
    inv_c = jnp.where(t_rows >= CMP_LEN - 1, 1.0 / jnp.maximum(l_c, TINY), 0.0)
    o_c = _dot(e_c.astype(BF16), vc_ref[...].astype(BF16)) * inv_c

    s_t = _dot_nt(kca, qa)
    e_t = jnp.exp2(s_t - jnp.max(s_t, axis=0, keepdims=True))
    l_t = jnp.sum(e_t, axis=0, keepdims=True)
    t_lane = start + (lax.broadcasted_iota(jnp.int32, (1, rows), 1) & (nq - 1))
    p_t = e_t * jnp.where(t_lane >= CMP_LEN - 1, 1.0 / jnp.maximum(l_t, TINY), 0.0)
    ps_t = []
    for g in range(N_KV):
        acc = p_t[:, (g * GROUP) * nq:(g * GROUP + 1) * nq]
        for r in range(1, GROUP):
            acc = acc + p_t[:, (g * GROUP + r) * nq:(g * GROUP + r + 1) * nq]
        ps_t.append(acc)
    ps_t = jnp.concatenate(ps_t, axis=1)
    imp_t = _split_dot(selmap_ref[...], ps_t)
    unsel = _unselected_blocks_t(imp_t, t_lane[:, :N_KV * nq], seq // SEL_LEN).T.astype(BF16)
    unsel_rows = jnp.concatenate([unsel[:nq]] * GROUP + [unsel[nq:]] * GROUP, axis=0)

    def online(s, vs, carry):
        m, l, acc = carry
        m_new = jnp.maximum(m, jnp.max(s, axis=-1, keepdims=True))
        alpha = jnp.exp2(m - m_new)
        p = jnp.exp2(s - m_new)
        l = alpha * l + jnp.sum(p, axis=-1, keepdims=True)
        acc = alpha * acc + _dot(p.astype(BF16), vs)
        return m_new, l, acc

    qs = jnp.concatenate([q2, unsel_rows], axis=1)

    def body(j, carry):
        k0 = pl.multiple_of(j * tk, tk)
        kt = jnp.concatenate([ksv_ref[pl.ds(k0, tk), 0:KV_W], selt_ref[j]], axis=1)
        return online(_dot_nt(qs, kt), ksv_ref[pl.ds(k0, tk), KV_W:2 * KV_W], carry)

    n_full = start // tk
    init = (jnp.full((rows, 1), -3.0e38, F32), jnp.zeros((rows, 1), F32), jnp.zeros((rows, KV_W), F32))
    carry = lax.fori_loop(0, n_full, body, init)
    k0 = pl.multiple_of(n_full * tk, tk)
    diag = iblk - n_full * (tk // nq)
    kt = jnp.concatenate([ksv_ref[pl.ds(k0, tk), 0:KV_W], selt_ref[n_full], cast_ref[diag]], axis=1)
    qd = jnp.concatenate([qs, onehot], axis=1)
    _, l_s, acc_s = online(_dot_nt(qd, kt), ksv_ref[pl.ds(k0, tk), KV_W:2 * KV_W], carry)
    o_s = acc_s * (1.0 / jnp.maximum(l_s, TINY))

    span = WINDOW + nq
    ws = pl.multiple_of(jnp.maximum(start - WINDOW, 0), nq)
    kt = jnp.concatenate([kwv_ref[pl.ds(ws, span), 0:KV_W], wint_ref[jnp.minimum(iblk, WINDOW // nq)]], axis=1)
    e_w, l_w = _exp2_rows(_dot_nt(qa, kt))
    o_w = _dot(e_w.astype(BF16), kwv_ref[pl.ds(ws, span), KV_W:2 * KV_W]) * (1.0 / jnp.maximum(l_w, TINY))

    o_ref[...] = _combine_heads(gate_ref[...], o_c, o_s, o_w, nq)


def _attn_prompt(q, gates, att_b, kcmp, vcmp, tables, batch, seq, tk):
    nb = seq // Q_BLOCK
    n_cmp = kcmp.shape[0] // batch
    cmpt, selt, cast, wint, onehot, selmap_t = tables
    blk = lambda b, i: (b * nb + i, 0)
    per_b = lambda b, i: (b, 0)
    const = lambda b, i: (0, 0)
    const3 = lambda b, i: (0, 0, 0)
    return pl.pallas_call(
        functools.partial(_attn_prompt_kernel, seq=seq, tk=tk),
        grid=(batch, nb),
        in_specs=[pl.BlockSpec((Q_BLOCK, ATTN_W), blk), pl.BlockSpec((Q_BLOCK, GATE_PAD), blk),
                  pl.BlockSpec((seq, 2 * KV_W), lambda b, i: (b, 0)),
                  pl.BlockSpec((seq, 2 * KV_W), lambda b, i: (b, 1)),
                  pl.BlockSpec((n_cmp, KV_W), per_b), pl.BlockSpec((n_cmp, KV_W), per_b),
                  pl.BlockSpec((None, n_cmp, LANES), lambda b, i: (i, 0, 0)),
                  pl.BlockSpec(selt.shape, const3), pl.BlockSpec(cast.shape, const3),
                  pl.BlockSpec(wint.shape, const3), pl.BlockSpec(onehot.shape, const),
                  pl.BlockSpec(selmap_t.shape, const)],
        out_specs=pl.BlockSpec((Q_BLOCK, ATTN_W), blk),
        out_shape=jax.ShapeDtypeStruct((batch * seq, ATTN_W), F32),
        compiler_params=_cparams("parallel", "parallel"),
        name="attn_prompt",
    )(q, gates, att_b, att_b, kcmp, vcmp, cmpt, selt, cast, wint, onehot, selmap_t)


def _attn_sample_kernel(pt_ref, q_ref, gate_ref, kvn_ref, winn_ref, wbuf_ref, *rest, n_pages, past):
    pages = rest[:n_pages]
    (wk_ref, wv_ref, bk_ref, bv_ref, cmpt_ref, selt_ref, newt_ref, wint_ref, onehot_ref, selmap_ref,
     o_ref, sh_ref, kraw_ref, vraw_ref) = rest[n_pages:]
    del pt_ref
    nq = q_ref.shape[0]
    page = pages[0].shape[1]
    n_rows = past // CMP_STRIDE
    q2 = _build_q2(q_ref[...], nq)
    onehot = onehot_ref[...]
    qa = jnp.concatenate([q2, onehot], axis=1)
    t_rows = _row_positions(nq, past)

    for p, pg in enumerate(pages):
        kraw_ref[p * page:(p + 1) * page, :] = pg[0:KV_W, :].T
        vraw_ref[p * page:(p + 1) * page, :] = pg[KV_W:2 * KV_W, :].T
    acc_k = jnp.zeros((n_rows, 2 * KV_W), F32)
    acc_v = jnp.zeros((n_rows, 2 * KV_W), F32)
    for j in range(CMP_STRIDE):
        rk = kraw_ref[pl.ds(j, n_rows, stride=CMP_STRIDE), :]
        rv = vraw_ref[pl.ds(j, n_rows, stride=CMP_STRIDE), :]
        acc_k = acc_k + _dot(rk.astype(BF16), wk_ref[j])
        acc_v = acc_v + _dot(rv.astype(BF16), wv_ref[j])
    sh_ref[0:n_rows, 0:KV_W] = acc_k[:, KV_W:]
    sh_ref[0:n_rows, KV_W:] = acc_v[:, KV_W:]
    sh_ref[n_rows:n_rows + SUBLANES, :] = jnp.zeros((SUBLANES, 2 * KV_W), F32)
    nxt = sh_ref[pl.ds(1, n_rows), :]
    kcmp = (acc_k[:, :KV_W] + nxt[:, :KV_W] + bk_ref[...]).astype(BF16)
    vcmp = (acc_v[:, :KV_W] + nxt[:, KV_W:] + bv_ref[...]).astype(BF16)

    e_c, l_c = _exp2_rows(_dot_nt(qa, jnp.concatenate([kcmp, cmpt_ref[...]], axis=1)))
    p_c = e_c * jnp.where(t_rows >= CMP_LEN - 1, 1.0 / jnp.maximum(l_c, TINY), 0.0)
    o_c = _dot(p_c.astype(BF16), vcmp)

    ps = []
    for g in range(N_KV):
        acc = p_c[(g * GROUP) * nq:(g * GROUP + 1) * nq]
        for r in range(1, GROUP):
            acc = acc + p_c[(g * GROUP + r) * nq:(g * GROUP + r + 1) * nq]
        ps.append(acc)
    ps = jnp.concatenate(ps, axis=0)
    ps_hi = ps.astype(BF16)
    ps_lo = (ps - ps_hi.astype(F32)).astype(BF16)
    imp = _dot(ps_hi, selmap_ref[...]) + _dot(ps_lo, selmap_ref[...])
    imp_t = jnp.concatenate([imp, jnp.zeros((LANES - N_KV * nq, LANES), F32)], axis=0).T
    t_lane = past + (lax.broadcasted_iota(jnp.int32, (1, LANES), 1) & (nq - 1))
    n_slc = -(-(past + nq) // SEL_LEN)
    unsel = _unselected_blocks_t(imp_t, t_lane, n_slc).T.astype(BF16)
    unsel_rows = jnp.concatenate([unsel[:nq]] * GROUP + [unsel[nq:N_KV * nq]] * GROUP, axis=0)

    zpad = jnp.zeros((page - nq, KV_W), F32)
    qs = jnp.concatenate([q2, unsel_rows], axis=1)
    s_parts = [_dot(qs, jnp.concatenate([pg[2 * KV_W:3 * KV_W, :].astype(BF16), selt_ref[p]], axis=0))
               for p, pg in enumerate(pages)]
    k_new = jnp.concatenate([kvn_ref[:, 2 * KV_W:3 * KV_W], zpad], axis=0).astype(BF16)
    v_new = jnp.concatenate([kvn_ref[:, 3 * KV_W:4 * KV_W], zpad], axis=0).astype(BF16)
    qn = jnp.concatenate([qs, onehot], axis=1)
    s_parts.append(_dot_nt(qn, jnp.concatenate([k_new, newt_ref[...]], axis=1)))
    e_s, l_s = _exp2_rows(jnp.concatenate(s_parts, axis=1))
    e_s = e_s.astype(BF16)
    acc = _dot(e_s[:, past:], v_new)
    for p, pg in enumerate(pages):
        acc = acc + _dot_nt(e_s[:, p * page:(p + 1) * page], pg[3 * KV_W:4 * KV_W, :].astype(BF16))
    o_s = acc * (1.0 / jnp.maximum(l_s, TINY))

    wb = wbuf_ref.shape[1]
    kw_new = jnp.concatenate([winn_ref[:, 0:KV_W], zpad], axis=0).astype(BF16)
    vw_new = jnp.concatenate([winn_ref[:, KV_W:], zpad], axis=0).astype(BF16)
    s_w = jnp.concatenate(
        [_dot(qa, jnp.concatenate([wbuf_ref[0:KV_W, :].astype(BF16), wint_ref[...]], axis=0)),
         _dot_nt(qa, jnp.concatenate([kw_new, newt_ref[:, LANES:]], axis=1))], axis=1)
    e_w, l_w = _exp2_rows(s_w)
    e_w = e_w.astype(BF16)
    o_w = (_dot_nt(e_w[:, :wb], wbuf_ref[KV_W:, :].astype(BF16)) + _dot(e_w[:, wb:], vw_new)) \
        * (1.0 / jnp.maximum(l_w, TINY))

    o_ref[...] = _combine_heads(gate_ref[...], o_c, o_s, o_w, nq)


def _attn_sample(page_table, q, gates, kvn, winn, wbuf_t, cache_t, layer, wk, wv, bk, bv, tables):
    db, nq, _ = q.shape
    n_pages = page_table.shape[1]
    page = cache_t.shape[3]
    past = n_pages * page
    wb = wbuf_t.shape[3]
    cmpt, selt, newt, wint, onehot, selmap = tables
    seq3 = lambda s, pt: (s, 0, 0)
    const2 = lambda s, pt: (0, 0)
    const3 = lambda s, pt: (0, 0, 0)

    def page_spec(p):
        return pl.BlockSpec((None, None, 4 * KV_W, page), lambda s, pt: (layer, pt[s, p], 0, 0))

    grid_spec = pltpu.PrefetchScalarGridSpec(
        num_scalar_prefetch=1,
        grid=(db,),
        in_specs=[pl.BlockSpec((None, nq, ATTN_W), seq3), pl.BlockSpec((None, nq, GATE_PAD), seq3),
                  pl.BlockSpec((None, nq, 4 * KV_W), seq3), pl.BlockSpec((None, nq, 2 * KV_W), seq3),
                  pl.BlockSpec((None, None, 2 * KV_W, wb), lambda s, pt: (layer, s, 0, 0))]
                 + [page_spec(p) for p in range(n_pages)]
                 + [pl.BlockSpec(wk.shape, const3), pl.BlockSpec(wv.shape, const3),
                    pl.BlockSpec((1, KV_W), const2), pl.BlockSpec((1, KV_W), const2),
                    pl.BlockSpec(cmpt.shape, const2), pl.BlockSpec(selt.shape, const3),
                    pl.BlockSpec(newt.shape, const2), pl.BlockSpec(wint.shape, const2),
                    pl.BlockSpec(onehot.shape, const2), pl.BlockSpec(selmap.shape, const2)],
        out_specs=pl.BlockSpec((None, nq, ATTN_W), seq3),
        scratch_shapes=[pltpu.VMEM((past // CMP_STRIDE + SUBLANES, 2 * KV_W), F32),
                        pltpu.VMEM((past, KV_W), F32), pltpu.VMEM((past, KV_W), F32)],
    )
    return pl.pallas_call(
        functools.partial(_attn_sample_kernel, n_pages=n_pages, past=past),
        grid_spec=grid_spec,
        out_shape=jax.ShapeDtypeStruct((db, nq, ATTN_W), F32),
        compiler_params=_cparams("parallel"),
        name="attn_sample",
    )(page_table, q, gates, kvn, winn, wbuf_t, *([cache_t] * n_pages), wk, wv, bk, bv,
      cmpt, selt, newt, wint, onehot, selmap)


def _ln_silu(y, g, b):
    mu = jnp.mean(y, axis=-1, keepdims=True)
    yc = y - mu
    var = jnp.mean(yc * yc, axis=-1, keepdims=True)
    z = yc * lax.rsqrt(var + EPS) * g + b
    return z * _sigmoid(z)


CONV_HALO = 32
CONV_ROWS = 64


def _conv_prompt_kernel(u_ref, prev_ref, w_ref, b_ref, lg_ref, lb_ref, o_ref, ext_ref, *, tiles_per_seq):
    tm = u_ref.shape[0]
    first = (pl.program_id(0) % tiles_per_seq) == 0
    ext_ref[0:CONV_HALO, :] = jnp.where(first, 0.0, prev_ref[...])
    ext_ref[CONV_HALO:CONV_HALO + tm, :] = u_ref[...]
    off = CONV_HALO - (CONV_WIDTH - 1)

    for r0 in range(0, tm, CONV_ROWS):
        acc = jnp.zeros((CONV_ROWS, u_ref.shape[1]), F32) + b_ref[...]
        for k in range(CONV_WIDTH):
            acc = acc + w_ref[k:k + 1, :] * ext_ref[r0 + off + k:r0 + off + k + CONV_ROWS, :]
        o_ref[r0:r0 + CONV_ROWS, :] = _ln_silu(acc, lg_ref[...], lb_ref[...])


def _conv_prompt(u, w, b, lg, lb, seq, tm):
    n, cc = u.shape
    const = lambda i: (0, 0)
    per_halo = tm // CONV_HALO
    return pl.pallas_call(
        functools.partial(_conv_prompt_kernel, tiles_per_seq=seq // tm),
        grid=(n // tm,),
        in_specs=[pl.BlockSpec((tm, cc), lambda i: (i, 0)),
                  pl.BlockSpec((CONV_HALO, cc), lambda i: (jnp.maximum(i * per_halo - 1, 0), 0)),
                  pl.BlockSpec(w.shape, const), pl.BlockSpec((1, cc), const),
                  pl.BlockSpec((1, cc), const), pl.BlockSpec((1, cc), const)],
        out_specs=pl.BlockSpec((tm, cc), lambda i: (i, 0)),
        out_shape=jax.ShapeDtypeStruct((n, cc), F32),
        scratch_shapes=[pltpu.VMEM((CONV_HALO + tm, cc), F32)],
        compiler_params=_cparams("parallel"),
        name="conv_prompt",
    )(u, u, w, b, lg, lb)


def _conv_sample_kernel(ext_ref, w_ref, b_ref, lg_ref, lb_ref, o_ref):
    ns, nq, cc = o_ref.shape
    for s in range(ns):
        acc = jnp.zeros((nq, cc), F32) + b_ref[...]
        for k in range(CONV_WIDTH):
            acc = acc + w_ref[k:k + 1, :] * ext_ref[s, pl.ds(k, nq), :]
        o_ref[s] = _ln_silu(acc, lg_ref[...], lb_ref[...])


def _conv_sample(ext, w, b, lg, lb, nq, ns):
    db, rows, cc = ext.shape
    const = lambda i: (0, 0)
    return pl.pallas_call(
        _conv_sample_kernel,
        grid=(db // ns,),
        in_specs=[pl.BlockSpec((ns, rows, cc), lambda i: (i, 0, 0)),
                  pl.BlockSpec(w.shape, const), pl.BlockSpec((1, cc), const),
                  pl.BlockSpec((1, cc), const), pl.BlockSpec((1, cc), const)],
        out_specs=pl.BlockSpec((ns, nq, cc), lambda i: (i, 0, 0)),
        out_shape=jax.ShapeDtypeStruct((db, nq, cc), F32),
        compiler_params=_cparams("parallel"),
        name="conv_sample",
    )(ext, w, b, lg, lb)


def _merge_kernel(x_ref, gpre_ref, oa_ref, oc_ref, wm_ref, wao_ref, wco_ref, wout_ref, gpost_ref, xo_ref):
    x = x_ref[...]
    d = x.shape[1]
    zb = _rms(x, gpre_ref[...]).astype(BF16)
    gm = _sigmoid(_dot(zb, wm_ref[...]))
    a = _dot(oa_ref[...].astype(BF16), wao_ref[...])
    c = _dot(oc_ref[...].astype(BF16), wco_ref[...])
    mix = gm[:, :d] * a + gm[:, d:] * c
    h = _dot(mix.astype(BF16), wout_ref[...])
    xo_ref[...] = x + _rms(h, gpost_ref[...])


def _merge(x, gpre, oa, oc, wm, wao, wco, wout, gpost, tm):
    n, d = x.shape
    const = lambda i: (0, 0)
    row = lambda i: (i, 0)
    return pl.pallas_call(
        _merge_kernel,
        grid=(n // tm,),
        in_specs=[pl.BlockSpec((tm, d), row), pl.BlockSpec((1, d), const),
                  pl.BlockSpec((tm, oa.shape[1]), row), pl.BlockSpec((tm, oc.shape[1]), row),
                  pl.BlockSpec(wm.shape, const), pl.BlockSpec(wao.shape, const),
                  pl.BlockSpec(wco.shape, const), pl.BlockSpec(wout.shape, const),
                  pl.BlockSpec((1, d), const)],
        out_specs=pl.BlockSpec((tm, d), row),
        out_shape=jax.ShapeDtypeStruct((n, d), F32),
        compiler_params=_cparams("parallel"),
        name="merge",
    )(x, gpre, oa, oc, wm, wao, wco, wout, gpost)


FFN_HALO = SUBLANES


def _ffn_kernel(*refs, sample, tiles_per_seq, tf, nq):
    if sample:
        (x_ref, gpre_ref, wg_ref, wv_ref, cw_ref, cb_ref, wd_ref, gpost_ref, h1_ref, h2_ref,
         xo_ref, up_ref, ext_ref) = refs
    else:
        (x_ref, gpre_ref, wg_ref, wv_ref, cw_ref, cb_ref, wd_ref, gpost_ref,
         xo_ref, up_ref, ext_ref, carry_ref) = refs
    x = x_ref[...]
    tm, d = x.shape
    dff = wd_ref.shape[0]
    zb = _rms(x, gpre_ref[...]).astype(BF16)
    first = (pl.program_id(0) % tiles_per_seq) == 0
    tpos = lax.broadcasted_iota(jnp.int32, (tm, 1), 0) & (nq - 1)
    if not sample:
        @pl.when(pl.program_id(0) == 0)
        def _():
            carry_ref[...] = jnp.zeros(carry_ref.shape, F32)

    def conv_half(w_ref, c0, col0):
        up = _dot(zb, w_ref[:, c0:c0 + tf])
        ext_ref[FFN_HALO:FFN_HALO + tm, :] = up
        if sample:
            ext_ref[0:FFN_HALO, :] = jnp.zeros((FFN_HALO, tf), F32)
            up_ref[:, col0:col0 + tf] = up
        else:
            ext_ref[0:FFN_HALO, :] = jnp.where(first, 0.0, carry_ref[:, col0:col0 + tf])
            carry_ref[:, col0:col0 + tf] = up[tm - FFN_HALO:, :]
            up_ref[:, col0:col0 + tf] = up[tm - FFN_HALO:, :]
        s1 = ext_ref[pl.ds(FFN_HALO - 1, tm), :]
        s2 = ext_ref[pl.ds(FFN_HALO - 2, tm), :]
        if sample:
            s1 = jnp.where(tpos == 0, h1_ref[:, col0:col0 + tf], s1)
            s2 = jnp.where(tpos <= 1, h2_ref[:, col0:col0 + tf], s2)
        cw = cw_ref[:, col0:col0 + tf]
        return cw[0:1] * s2 + cw[1:2] * s1 + cw[2:3] * up + cb_ref[:, col0:col0 + tf]

    acc = jnp.zeros((tm, d), F32)
    for c in range(dff // tf):
        hg = conv_half(wg_ref, c * tf, c * tf)
        hv = conv_half(wv_ref, c * tf, dff + c * tf)
        act = (_gelu_tanh(hg) * hv).astype(BF16)
        acc = acc + _dot(act, wd_ref[c * tf:(c + 1) * tf, :])
    xo_ref[...] = x + _rms(acc, gpost_ref[...])


def _ffn(x, gpre, wg, wv, cw, cb, wd, gpost, tm, tf, seq=None, hist=None, nq=None):
    n, d = x.shape
    dff = wd.shape[0]
    sample = hist is not None
    const = lambda i: (0, 0)
    row = lambda i: (i, 0)
    single = pl.Buffered(1)
    in_specs = [pl.BlockSpec((tm, d), row), pl.BlockSpec((1, d), const),
                pl.BlockSpec(wg.shape, const, pipeline_mode=single),
                pl.BlockSpec(wv.shape, const, pipeline_mode=single),
                pl.BlockSpec(cw.shape, const), pl.BlockSpec(cb.shape, const),
                pl.BlockSpec(wd.shape, const, pipeline_mode=single), pl.BlockSpec((1, d), const)]
    args = [x, gpre, wg, wv, cw, cb, wd, gpost]
    scratch = [pltpu.VMEM((FFN_HALO + tm, tf), F32)]
    if sample:
        in_specs += [pl.BlockSpec((tm, 2 * dff), row)] * 2
        args += list(hist)
        up_rows, tiles_per_seq = tm, 1
    else:
        scratch.append(pltpu.VMEM((FFN_HALO, 2 * dff), F32))
        up_rows, tiles_per_seq, nq = FFN_HALO, seq // tm, tm
    return pl.pallas_call(
        functools.partial(_ffn_kernel, sample=sample, tiles_per_seq=tiles_per_seq, tf=tf, nq=nq),
        grid=(n // tm,),
        in_specs=in_specs,
        out_specs=[pl.BlockSpec((tm, d), row), pl.BlockSpec((up_rows, 2 * dff), row)],
        out_shape=[jax.ShapeDtypeStruct((n, d), F32),
                   jax.ShapeDtypeStruct((n // tm * up_rows, 2 * dff), F32)],
        scratch_shapes=scratch,
        compiler_params=_cparams("arbitrary"),
        name="ffn_sample" if sample else "ffn_prompt",
    )(*args)


def _bias(cond):
    return jnp.asarray(np.where(cond, NEG_BIG, 0.0).astype(np.float32), dtype=BF16)


def _onehot_rows(rows, nq):
    return jnp.asarray((np.arange(LANES)[None, :] == (np.arange(rows)[:, None] % nq)).astype(np.float32), dtype=BF16)


def _cmp_sel_table(n_rows, n_cmp, n_slc):
    s = np.arange(n_rows) * CMP_STRIDE
    e = s + CMP_LEN - 1
    js = np.arange(LANES) * SEL_LEN
    je = js + SEL_LEN - 1
    m = (s[:, None] <= je[None, :]) & (e[:, None] >= js[None, :])
    m &= (np.arange(n_rows)[:, None] < n_cmp) & (np.arange(LANES)[None, :] < n_slc)
    return m.astype(np.float32)


def _prompt_tables(seq, tk):
    nq = Q_BLOCK
    n_rows = seq // CMP_STRIDE
    lane = np.arange(LANES)
    n = np.arange(n_rows)
    blk = np.arange(seq // nq)
    cmpt = _bias(CMP_STRIDE * n[None, :, None] + CMP_LEN - 1 > nq * blk[:, None, None] + lane[None, None, :])
    key = np.arange(seq).reshape(seq // tk, tk)
    selt = _bias(key[:, :, None] // SEL_LEN == lane[None, None, :])
    i = np.arange(tk)
    d = np.arange(tk // nq)
    cast = _bias(i[None, :, None] > nq * d[:, None, None] + lane[None, None, :])
    span = np.arange(WINDOW + nq)
    w = np.arange(WINDOW // nq + 1)
    dt = (nq * w[:, None, None] + lane[None, None, :]) - span[None, :, None]
    wint = _bias((dt < 0) | (dt > WINDOW))
    n_cmp = (seq - CMP_LEN) // CMP_STRIDE + 1
    selmap_t = jnp.asarray(_cmp_sel_table(n_rows, n_cmp, seq // SEL_LEN).T, dtype=BF16)
    return cmpt, selt, cast, wint, _onehot_rows(N_HEADS * nq, nq), selmap_t


def _sample_tables(past, page, nq, wb):
    n_rows = past // CMP_STRIDE
    lane = np.arange(LANES)
    n = np.arange(n_rows)
    qvalid = lane < nq
    cmpt = _bias((CMP_STRIDE * n[:, None] + CMP_LEN - 1 > past + lane[None, :]) & qvalid[None, :])
    key = np.arange(past).reshape(past // page, 1, page)
    selt = _bias(key // SEL_LEN == lane[None, :, None])
    i = np.arange(page)
    new_sel = (past + i[:, None]) // SEL_LEN == lane[None, :]
    new_causal = (i[:, None] > lane[None, :]) & qvalid[None, :]
    newt = _bias(np.concatenate([new_sel, new_causal], axis=1))
    dt = (lane[:, None] + wb) - np.arange(wb)[None, :]
    wint = _bias(((dt < 0) | (dt > WINDOW)) & qvalid[:, None])
    n_cmp = (past + nq - CMP_LEN) // CMP_STRIDE + 1
    selmap = jnp.asarray(_cmp_sel_table(n_rows, n_cmp, -(-(past + nq) // SEL_LEN)), dtype=BF16)
    return cmpt, selt, newt, wint, _onehot_rows(N_HEADS * nq, nq), selmap


def _phi_weights(w):
    depth = w.shape[0]
    wj = w.reshape(depth, CMP_LEN, HEAD_DIM, HEAD_DIM)
    eye = jnp.eye(N_KV, dtype=w.dtype)
    bd = jnp.einsum("gh,ljde->ljgdhe", eye, wj).reshape(depth, CMP_LEN, KV_W, KV_W)
    return jnp.concatenate([bd[:, :CMP_STRIDE], bd[:, CMP_STRIDE:]], axis=-1).astype(BF16)


def kernel(x_prompt, x_sample, cache_kv, state_kv_win, state_conv, state_ffn, page_table, norm_mix_pre,
           norm_mix_post, norm_ffn_pre, norm_ffn_post, w_in, w_phi_k, b_phi_k, w_phi_v, b_phi_v, w_attn_out,
           conv_w, conv_b, conv_ln_g, conv_ln_b, w_conv_out, w_out, w_up, ffn_conv_w, ffn_conv_b, w_down):
    batch, seq, d = x_prompt.shape
    db, nq, _ = x_sample.shape
    depth, n_phys, page = cache_kv.shape[:3]
    n_pages = page_table.shape[1]
    past = n_pages * page
    wb = state_kv_win.shape[2]
    cc = conv_w.shape[2]
    dff = w_down.shape[1]
    assert seq % Q_BLOCK == 0 and nq == SUBLANES and page == LANES and wb == WINDOW

    tm = 512
    tm_s = 128
    tf = 256
    tk = 512
    assert dff % tf == 0 and seq % tk == 0 and tk % Q_BLOCK == 0 and WINDOW % Q_BLOCK == 0

    c1 = ATTN_W + 6 * KV_W
    c2 = c1 + 3 * N_HEADS
    c3 = c2 + 2 * cc
    wq = w_in[:, :, :c1].astype(BF16)
    wgate = jnp.pad(w_in[:, :, c1:c2], ((0, 0), (0, 0), (0, GATE_PAD - 3 * N_HEADS))).astype(BF16)
    wu = w_in[:, :, c2:c3].astype(BF16)
    wm = w_in[:, :, c3:].astype(BF16)
    wao = w_attn_out.astype(BF16)
    wco = w_conv_out.astype(BF16)
    wout = w_out.astype(BF16)
    wup_g = w_up[:, :, :dff].astype(BF16)
    wup_v = w_up[:, :, dff:].astype(BF16)
    wdn = w_down.astype(BF16)
    wk_ab = _phi_weights(w_phi_k)
    wv_ab = _phi_weights(w_phi_v)
    bk2 = jnp.tile(b_phi_k, (1, N_KV)).reshape(depth, 1, KV_W)
    bv2 = jnp.tile(b_phi_v, (1, N_KV)).reshape(depth, 1, KV_W)

    tables_p = _prompt_tables(seq, tk)
    tables_s = _sample_tables(past, page, nq, wb)

    cache_t = cache_kv.transpose(0, 1, 3, 4, 5, 2).reshape(depth, n_phys, 4 * KV_W, page)
    wbuf_t = state_kv_win.transpose(0, 1, 3, 4, 5, 2).reshape(depth, db, 2 * KV_W, wb)
    row = lambda v: v.reshape(1, -1)

    xp = x_prompt.reshape(batch * seq, d)
    xs = x_sample.reshape(db * nq, d)
    outs = [[] for _ in range(8)]
    for l in range(depth):
        gpre, gpost = row(norm_mix_pre[l]), row(norm_mix_post[l])
        fpre, fpost = row(norm_ffn_pre[l]), row(norm_ffn_post[l])
        cb, lg, lb = row(conv_b[l]), row(conv_ln_g[l]), row(conv_ln_b[l])
        fcb = row(ffn_conv_b[l])

        q, kv, win, att_b, gates, u = _in_proj(xp, gpre, wq[l], wgate[l], wu[l], tm)
        kcmp, vcmp = _compress_prompt(kv, wk_ab[l], wv_ab[l], bk2[l], bv2[l], batch, seq)
        oa = _attn_prompt(q, gates, att_b, kcmp, vcmp, tables_p, batch, seq, tk)
        oc = _conv_prompt(u, conv_w[l], cb, lg, lb, seq, tm)
        xp = _merge(xp, gpre, oa, oc, wm[l], wao[l], wco[l], wout[l], gpost, tm)
        xp, up_tail = _ffn(xp, fpre, wup_g[l], wup_v[l], ffn_conv_w[l], fcb, wdn[l], fpost, tm, tf, seq=seq)
        outs[0].append(kv.reshape(batch, seq, 4, N_KV, HEAD_DIM))
        outs[2].append(win.reshape(batch, seq, 2, N_KV, HEAD_DIM)[:, seq - min(WINDOW, seq):])
        outs[4].append(u.reshape(batch, seq, cc)[:, seq - (CONV_WIDTH - 1):])
        outs[6].append(up_tail.reshape(batch, seq // tm, FFN_HALO, 2 * dff)[:, -1, FFN_HALO - (FFN_CONV_WIDTH - 1):])

        q, kv, win, _, gates, u = _in_proj(xs, gpre, wq[l], wgate[l], wu[l], tm)
        oa = _attn_sample(page_table, q.reshape(db, nq, ATTN_W), gates.reshape(db, nq, GATE_PAD),
                          kv.reshape(db, nq, 4 * KV_W), win.reshape(db, nq, 2 * KV_W), wbuf_t, cache_t, l,
                          wk_ab[l], wv_ab[l], bk2[l], bv2[l], tables_s)
        ext = jnp.concatenate([state_conv[l], u.reshape(db, nq, cc)], axis=1)
        oc = _conv_sample(ext, conv_w[l], cb, lg, lb, nq, 16)
        xs = _merge(xs, gpre, oa.reshape(db * nq, ATTN_W), oc.reshape(db * nq, cc),
                    wm[l], wao[l], wco[l], wout[l], gpost, tm)
        hist = state_ffn[l]
        h2 = jnp.pad(hist, ((0, 0), (0, nq - (FFN_CONV_WIDTH - 1)), (0, 0))).reshape(db * nq, 2 * dff)
        h1 = jnp.pad(hist[:, 1:], ((0, 0), (0, nq - 1), (0, 0))).reshape(db * nq, 2 * dff)
        xs, up_s = _ffn(xs, fpre, wup_g[l], wup_v[l], ffn_conv_w[l], fcb, wdn[l], fpost, tm_s, tf,
                        hist=(h1, h2), nq=nq)
        outs[1].append(kv.reshape(db, nq, 4, N_KV, HEAD_DIM))
        win_all = jnp.concatenate([state_kv_win[l], win.reshape(db, nq, 2, N_KV, HEAD_DIM)], axis=1)
        outs[3].append(win_all[:, win_all.shape[1] - WINDOW:])
        outs[5].append(ext[:, nq:])
        outs[7].append(up_s.reshape(db, nq, 2 * dff)[:, nq - (FFN_CONV_WIDTH - 1):])

    return (xp.reshape(batch, seq, d), xs.reshape(db, nq, d),
            jnp.stack(outs[0]), jnp.stack(outs[1]), jnp.stack(outs[2]), jnp.stack(outs[3]),
            jnp.stack(outs[4]), jnp.stack(outs[5]), jnp.stack(outs[6]), jnp.stack(outs[7]))
```

```python
import functools

import numpy as np
import jax
import jax.numpy as jnp
from jax import lax
from jax.experimental import pallas as pl
from jax.experimental.pallas import tpu as pltpu

F32 = jnp.float32
BF16 = jnp.bfloat16

N_HEADS = 8
N_KV = 2
GROUP = N_HEADS // N_KV
HEAD_DIM = 64
ATTN_W = N_HEADS * HEAD_DIM
KV_W = N_KV * HEAD_DIM
CMP_LEN = 32
CMP_STRIDE = 16
SEL_LEN = 64
N_SEL = 8
WINDOW = 512
Q_BLOCK = 128
FORCE_BONUS = 1000.0
CONV_WIDTH = 31
FFN_CONV_WIDTH = 3
EPS = 1e-6
TINY = 1e-30
NEG_BIG = -(2.0 ** 100)
LOG2E = 1.4426950408889634
GATE_PAD = 128

LANES = 128
SUBLANES = 8
VMEM_LIMIT = 56 * 1024 * 1024


def _cparams(*sem):
    return pltpu.CompilerParams(dimension_semantics=sem, vmem_limit_bytes=VMEM_LIMIT)


def _rms(x, g):
    return x * lax.rsqrt(jnp.mean(x * x, axis=-1, keepdims=True) + EPS) * g


def _sigmoid(x):
    return 1.0 / (1.0 + jnp.exp(-x))


def _gelu_tanh(x):
    c = np.sqrt(2.0 / np.pi).astype(np.float32)
    return x * (0.5 * (1.0 + jnp.tanh(c * (x + 0.044715 * (x * x * x)))))


def _dot(a, b):
    return jnp.dot(a, b, preferred_element_type=F32)


def _dot_nt(a, b):
    return lax.dot_general(a, b, (((1,), (1,)), ((), ())), preferred_element_type=F32)


def _exp2_rows(s):
    m = jnp.max(s, axis=-1, keepdims=True)
    e = jnp.exp2(s - m)
    return e, jnp.sum(e, axis=-1, keepdims=True)


def _build_q2(q, nq):
    lane = lax.broadcasted_iota(jnp.int32, (nq, LANES), 1)
    parts = []
    for h in range(N_HEADS):
        g = h // GROUP
        c = h // 2
        chunk = q[:, c * LANES:(c + 1) * LANES]
        if (h % 2) != g:
            chunk = pltpu.roll(chunk, HEAD_DIM, 1)
        keep = (lane >= g * HEAD_DIM) & (lane < (g + 1) * HEAD_DIM)
        parts.append(jnp.where(keep, chunk * (HEAD_DIM ** -0.5 * LOG2E), 0.0))
    return jnp.concatenate(parts, axis=0).astype(BF16)


def _row_positions(nq, t0):
    r = lax.broadcasted_iota(jnp.int32, (N_HEADS * nq, 1), 0)
    return t0 + (r & (nq - 1))


def _unselected_blocks_t(imp_t, t_lane, n_slc):
    jb = lax.broadcasted_iota(jnp.int32, imp_t.shape, 0)
    cur = t_lane >> 6
    valid = (jb * SEL_LEN) <= t_lane
    forced = (jb == 0) | (jb == cur) | (jb == cur - 1)
    score = jnp.where(valid, imp_t + FORCE_BONUS * forced.astype(F32), -1.0)
    score = jnp.where(jb < n_slc, score, -2.0)
    jbf = jb.astype(F32)
    unsel = jnp.ones(imp_t.shape, F32)
    for _ in range(N_SEL):
        m = jnp.max(score, axis=0, keepdims=True)
        idx = jnp.min(jnp.where(score == m, jbf, 1e9), axis=0, keepdims=True)
        hit = jbf == idx
        unsel = jnp.where(hit, 0.0, unsel)
        score = jnp.where(hit, -3.0, score)
    return unsel


def _split_dot(w, x):
    hi = x.astype(BF16)
    lo = (x - hi.astype(F32)).astype(BF16)
    return _dot(w, hi) + _dot(w, lo)


def _combine_heads(gates, o_c, o_s, o_w, nq):
    lane = lax.broadcasted_iota(jnp.int32, (nq, LANES), 1)
    heads = []
    for h in range(N_HEADS):
        sl = slice(h * nq, (h + 1) * nq)
        o = (gates[:, h:h + 1] * o_c[sl]
             + gates[:, N_HEADS + h:N_HEADS + h + 1] * o_s[sl]
             + gates[:, 2 * N_HEADS + h:2 * N_HEADS + h + 1] * o_w[sl])
        if (h % 2) != (h // GROUP):
            o = pltpu.roll(o, HEAD_DIM, 1)
        heads.append(o)
    chunks = [jnp.where(lane < HEAD_DIM, heads[2 * c], heads[2 * c + 1]) for c in range(N_HEADS // 2)]
    return jnp.concatenate(chunks, axis=1)


def _in_proj_kernel(x_ref, g_ref, wq_ref, wg_ref, wu_ref, q_ref, kv_ref, win_ref, att_ref, gate_ref, u_ref):
    zb = _rms(x_ref[...], g_ref[...]).astype(BF16)
    qkv = _dot(zb, wq_ref[...])
    q_ref[...] = qkv[:, :ATTN_W]
    kv_ref[...] = qkv[:, ATTN_W:ATTN_W + 4 * KV_W]
    win_ref[...] = qkv[:, ATTN_W + 4 * KV_W:]
    att_ref[...] = qkv[:, ATTN_W + 2 * KV_W:].astype(BF16)
    gate_ref[...] = _sigmoid(_dot(zb, wg_ref[...]))
    u2 = _dot(zb, wu_ref[...])
    cc = u2.shape[1] // 2
    u_ref[...] = u2[:, :cc] * _sigmoid(u2[:, cc:])


def _in_proj(x, g, wq, wg, wu, tm):
    n, d = x.shape
    cc = wu.shape[1] // 2
    const = lambda i: (0, 0)
    row = lambda i: (i, 0)
    return pl.pallas_call(
        _in_proj_kernel,
        grid=(n // tm,),
        in_specs=[pl.BlockSpec((tm, d), row), pl.BlockSpec((1, d), const),
                  pl.BlockSpec(wq.shape, const), pl.BlockSpec(wg.shape, const), pl.BlockSpec(wu.shape, const)],
        out_specs=[pl.BlockSpec((tm, ATTN_W), row), pl.BlockSpec((tm, 4 * KV_W), row),
                   pl.BlockSpec((tm, 2 * KV_W), row), pl.BlockSpec((tm, 4 * KV_W), row),
                   pl.BlockSpec((tm, GATE_PAD), row),
                   pl.BlockSpec((tm, cc), row)],
        out_shape=[jax.ShapeDtypeStruct((n, ATTN_W), F32), jax.ShapeDtypeStruct((n, 4 * KV_W), F32),
                   jax.ShapeDtypeStruct((n, 2 * KV_W), F32), jax.ShapeDtypeStruct((n, 4 * KV_W), BF16),
                   jax.ShapeDtypeStruct((n, GATE_PAD), F32),
                   jax.ShapeDtypeStruct((n, cc), F32)],
        compiler_params=_cparams("parallel"),
        name="in_proj",
    )(x, g, wq, wg, wu)


def _compress_prompt_kernel(k_ref, v_ref, wk_ref, wv_ref, bk_ref, bv_ref, kc_ref, vc_ref, sh_ref, *, n_rows):
    acc_k = jnp.zeros((n_rows, 2 * KV_W), F32)
    acc_v = jnp.zeros((n_rows, 2 * KV_W), F32)
    for j in range(CMP_STRIDE):
        rk = k_ref[pl.ds(j, n_rows, stride=CMP_STRIDE), :].astype(BF16)
        rv = v_ref[pl.ds(j, n_rows, stride=CMP_STRIDE), :].astype(BF16)
        acc_k = acc_k + _dot(rk, wk_ref[j])
        acc_v = acc_v + _dot(rv, wv_ref[j])
    sh_ref[0:n_rows, 0:KV_W] = acc_k[:, KV_W:]
    sh_ref[0:n_rows, KV_W:] = acc_v[:, KV_W:]
    sh_ref[n_rows:n_rows + SUBLANES, :] = jnp.zeros((SUBLANES, 2 * KV_W), F32)
    nxt = sh_ref[pl.ds(1, n_rows), :]
    kc_ref[...] = acc_k[:, :KV_W] + nxt[:, :KV_W] + bk_ref[...]
    vc_ref[...] = acc_v[:, :KV_W] + nxt[:, KV_W:] + bv_ref[...]


def _compress_prompt(kv, wk, wv, bk, bv, batch, seq):
    n_rows = seq // CMP_STRIDE
    const2 = lambda b: (0, 0)
    const3 = lambda b: (0, 0, 0)
    return pl.pallas_call(
        functools.partial(_compress_prompt_kernel, n_rows=n_rows),
        grid=(batch,),
        in_specs=[pl.BlockSpec((seq, KV_W), lambda b: (b, 0)), pl.BlockSpec((seq, KV_W), lambda b: (b, 1)),
                  pl.BlockSpec(wk.shape, const3), pl.BlockSpec(wv.shape, const3),
                  pl.BlockSpec((1, KV_W), const2), pl.BlockSpec((1, KV_W), const2)],
        out_specs=[pl.BlockSpec((n_rows, KV_W), lambda b: (b, 0))] * 2,
        out_shape=[jax.ShapeDtypeStruct((batch * n_rows, KV_W), F32)] * 2,
        scratch_shapes=[pltpu.VMEM((n_rows + SUBLANES, 2 * KV_W), F32)],
        compiler_params=_cparams("parallel"),
        name="compress_prompt",
    )(kv, kv, wk, wv, bk, bv)


def _softmax_pv(s, v_aug):
    m = jnp.max(s, axis=-1, keepdims=True)
    return _dot(jnp.exp2(s - m), v_aug)


def _group_values(v):
    lane = lax.broadcasted_iota(jnp.int32, v.shape, 1)
    one = jnp.ones(v.shape, v.dtype)
    return jnp.where(lane < HEAD_DIM, v, one), jnp.where(lane >= HEAD_DIM, v, one)


def _normalise(acc):
    return acc * (1.0 / jnp.maximum(pltpu.roll(acc, HEAD_DIM, 1), TINY))


def _attn_prompt_kernel(q_ref, gate_ref, ksv_ref, kwv_ref, kc_ref, vc_ref, cmpt_ref, selt_ref, cast_ref,
                        wint_ref, onehot_ref, selmap_ref, o_ref, *, seq, tk):
    nq = Q_BLOCK
    rows = N_HEADS * nq
    half = rows // N_KV
    iblk = pl.program_id(1)
    start = iblk * nq
    q2 = _build_q2(q_ref[...], nq)
    onehot = onehot_ref[...]
    qa = jnp.concatenate([q2, onehot], axis=1)
    t_rows = _row_positions(nq, start)

    def two_groups(s, v):
        v0, v1 = _group_values(v)
        return jnp.concatenate([_softmax_pv(s[:half], v0), _softmax_pv(s[half:], v1)], axis=0)

    kca = jnp.concatenate([kc_ref[...].astype(BF16), cmpt_ref[...]], axis=1)
    s_c = _dot_nt(qa, kca).astype(BF16)
    o_c = jnp.where(t_rows >= CMP_LEN - 1, _normalise(two_groups(s_c, vc_ref[...].astype(BF16))), 0.0)

    s_t = _dot_nt(kca, qa)
    e_t = jnp.exp2(s_t - jnp.max(s_t, axis=0, keepdims=True))
    l_t = jnp.sum(e_t, axis=0, keepdims=True)
    t_lane = start + (lax.broadcasted_iota(jnp.int32, (1, rows), 1) & (nq - 1))
    p_t = e_t * jnp.where(t_lane >= CMP_LEN - 1, 1.0 / jnp.maximum(l_t, TINY), 0.0)
    ps_t = []
    for g in range(N_KV):
        acc = p_t[:, (g * GROUP) * nq:(g * GROUP + 1) * nq]
        for r in range(1, GROUP):
            acc = acc + p_t[:, (g * GROUP + r) * nq:(g * GROUP + r + 1) * nq]
        ps_t.append(acc)
    ps_t = jnp.concatenate(ps_t, axis=1)
    imp_t = _split_dot(selmap_ref[...], ps_t)
    unsel = _unselected_blocks_t(imp_t, t_lane[:, :N_KV * nq], seq // SEL_LEN).T

    lane_q = lax.broadcasted_iota(jnp.int32, (half, LANES), 1)
    lane_k = lax.broadcasted_iota(jnp.int32, (tk, LANES), 1)
    bits0 = pltpu.roll(unsel[:nq], HEAD_DIM, 1).astype(BF16)
    bits1 = unsel[nq:].astype(BF16)
    lhs = (jnp.where(lane_q < HEAD_DIM, q2[:half], jnp.concatenate([bits0] * GROUP, axis=0)),
           jnp.where(lane_q >= HEAD_DIM, q2[half:], jnp.concatenate([bits1] * GROUP, axis=0)))

    lhs = tuple(jnp.concatenate([x, onehot[:half]], axis=1) for x in lhs)
    n_full = start // tk

    def scores(j):
        k0 = pl.multiple_of(j * tk, tk)
        ks = ksv_ref[pl.ds(k0, tk), 0:KV_W]
        tab = selt_ref[j]
        causal = cast_ref[jnp.where(j == n_full, iblk - n_full * (tk // nq), tk // nq)]
        rhs = (jnp.concatenate([jnp.where(lane_k < HEAD_DIM, ks, tab), causal], axis=1),
               jnp.concatenate([jnp.where(lane_k >= HEAD_DIM, ks, tab), causal], axis=1))
        return tuple(_dot_nt(lhs[g], rhs[g]).astype(BF16) for g in range(N_KV))

    def consume(j, s, carry):
        k0 = pl.multiple_of(j * tk, tk)
        vals = _group_values(ksv_ref[pl.ds(k0, tk), KV_W:2 * KV_W])
        out = []
        for g in range(N_KV):
            m, acc = carry[g]
            m_new = jnp.maximum(m, jnp.max(s[g], axis=-1, keepdims=True).astype(F32))
            p = jnp.exp2(s[g] - m_new.astype(BF16))
            out.append((m_new, jnp.exp2(m - m_new) * acc + _dot(p, vals[g])))
        return tuple(out)

    def body(j, state):
        s, carry = state
        return scores(j + 1), consume(j, s, carry)

    init = tuple((jnp.full((half, 1), -3.0e38, F32), jnp.zeros((half, KV_W), F32)) for _ in range(N_KV))
    s_last, carry = lax.fori_loop(0, n_full, body, (scores(0), init))
    carry = consume(n_full, s_last, carry)
    o_s = _normalise(jnp.concatenate([carry[g][1] for g in range(N_KV)], axis=0))

    span = WINDOW + nq
    ws = pl.multiple_of(jnp.maximum(start - WINDOW, 0), nq)
    kt = jnp.concatenate([kwv_ref[pl.ds(ws, span), 0:KV_W], wint_ref[jnp.minimum(iblk, WINDOW // nq)]], axis=1)
    s_w = _dot_nt(qa, kt).astype(BF16)
    o_w = _normalise(two_groups(s_w, kwv_ref[pl.ds(ws, span), KV_W:2 * KV_W]))

    o_ref[...] = _combine_heads(gate_ref[...], o_c, o_s, o_w, nq)


def _attn_prompt(q, gates, att_b, kcmp, vcmp, tables, batch, seq, tk):
    nb = seq // Q_BLOCK
    n_cmp = kcmp.shape[0] // batch
    cmpt, selt, cast, wint, onehot, selmap_t = tables
    blk = lambda b, i: (b * nb + i, 0)
    per_b = lambda b, i: (b, 0)
    const = lambda b, i: (0, 0)
    const3 = lambda b, i: (0, 0, 0)
    return pl.pallas_call(
        functools.partial(_attn_prompt_kernel, seq=seq, tk=tk),
        grid=(batch, nb),
        in_specs=[pl.BlockSpec((Q_BLOCK, ATTN_W), blk), pl.BlockSpec((Q_BLOCK, GATE_PAD), blk),
                  pl.BlockSpec((seq, 2 * KV_W), lambda b, i: (b, 0)),
                  pl.BlockSpec((seq, 2 * KV_W), lambda b, i: (b, 1)),
                  pl.BlockSpec((n_cmp, KV_W), per_b), pl.BlockSpec((n_cmp, KV_W), per_b),
                  pl.BlockSpec((None, n_cmp, LANES), lambda b, i: (i, 0, 0)),
                  pl.BlockSpec(selt.shape, const3), pl.BlockSpec(cast.shape, const3),
                  pl.BlockSpec(wint.shape, const3), pl.BlockSpec(onehot.shape, const),
                  pl.BlockSpec(selmap_t.shape, const)],
        out_specs=pl.BlockSpec((Q_BLOCK, ATTN_W), blk),
        out_shape=jax.ShapeDtypeStruct((batch * seq, ATTN_W), F32),
        compiler_params=_cparams("parallel", "parallel"),
        name="attn_prompt",
    )(q, gates, att_b, att_b, kcmp, vcmp, cmpt, selt, cast, wint, onehot, selmap_t)


def _attn_sample_kernel(pt_ref, q_ref, gate_ref, kvn_ref, winn_ref, wbuf_ref, *rest, n_pages, past, ns):
    del pt_ref
    consts = rest[ns * n_pages:ns * n_pages + 10]
    o_ref, sh_ref, kraw_ref, vraw_ref = rest[ns * n_pages + 10:]
    for u in range(ns):
        _attn_sample_one(q_ref.at[u], gate_ref.at[u], kvn_ref.at[u], winn_ref.at[u], wbuf_ref.at[u],
                         rest[u * n_pages:(u + 1) * n_pages], consts, o_ref.at[u], sh_ref.at[u],
                         kraw_ref.at[u], vraw_ref.at[u], past)


def _attn_sample_one(q_ref, gate_ref, kvn_ref, winn_ref, wbuf_ref, pages, consts, o_ref, sh_ref, kraw_ref,
                     vraw_ref, past):
    wk_ref, wv_ref, bk_ref, bv_ref, cmpt_ref, selt_ref, newt_ref, wint_ref, onehot_ref, selmap_ref = consts
    nq = q_ref.shape[0]
    page = pages[0].shape[1]
    n_rows = past // CMP_STRIDE
    q2 = _build_q2(q_ref[...], nq)
    onehot = onehot_ref[...]
    qa = jnp.concatenate([q2, onehot], axis=1)
    t_rows = _row_positions(nq, past)

    for p, pg in enumerate(pages):
        kraw_ref[p * page:(p + 1) * page, :] = pg[0:KV_W, :].T
        vraw_ref[p * page:(p + 1) * page, :] = pg[KV_W:2 * KV_W, :].T
    acc_k = jnp.zeros((n_rows, 2 * KV_W), F32)
    acc_v = jnp.zeros((n_rows, 2 * KV_W), F32)
    for j in range(CMP_STRIDE):
        rk = kraw_ref[pl.ds(j, n_rows, stride=CMP_STRIDE), :]
        rv = vraw_ref[pl.ds(j, n_rows, stride=CMP_STRIDE), :]
        acc_k = acc_k + _dot(rk.astype(BF16), wk_ref[j])
        acc_v = acc_v + _dot(rv.astype(BF16), wv_ref[j])
    sh_ref[0:n_rows, 0:KV_W] = acc_k[:, KV_W:]
    sh_ref[0:n_rows, KV_W:] = acc_v[:, KV_W:]
    sh_ref[n_rows:n_rows + SUBLANES, :] = jnp.zeros((SUBLANES, 2 * KV_W), F32)
    nxt = sh_ref[pl.ds(1, n_rows), :]
    kcmp = (acc_k[:, :KV_W] + nxt[:, :KV_W] + bk_ref[...]).astype(BF16)
    vcmp = (acc_v[:, :KV_W] + nxt[:, KV_W:] + bv_ref[...]).astype(BF16)

    e_c, l_c = _exp2_rows(_dot_nt(qa, jnp.concatenate([kcmp, cmpt_ref[...]], axis=1)))
    p_c = e_c * jnp.where(t_rows >= CMP_LEN - 1, 1.0 / jnp.maximum(l_c, TINY), 0.0)
    o_c = _dot(p_c.astype(BF16), vcmp)

    ps = []
    for g in range(N_KV):
        acc = p_c[(g * GROUP) * nq:(g * GROUP + 1) * nq]
        for r in range(1, GROUP):
            acc = acc + p_c[(g * GROUP + r) * nq:(g * GROUP + r + 1) * nq]
        ps.append(acc)
    ps = jnp.concatenate(ps, axis=0)
    ps_hi = ps.astype(BF16)
    ps_lo = (ps - ps_hi.astype(F32)).astype(BF16)
    imp = _dot(ps_hi, selmap_ref[...]) + _dot(ps_lo, selmap_ref[...])
    imp_t = jnp.concatenate([imp, jnp.zeros((LANES - N_KV * nq, LANES), F32)], axis=0).T
    t_lane = past + (lax.broadcasted_iota(jnp.int32, (1, LANES), 1) & (nq - 1))
    n_slc = -(-(past + nq) // SEL_LEN)
    unsel = _unselected_blocks_t(imp_t, t_lane, n_slc).T.astype(BF16)
    unsel_rows = jnp.concatenate([unsel[:nq]] * GROUP + [unsel[nq:N_KV * nq]] * GROUP, axis=0)

    zpad = jnp.zeros((page - nq, KV_W), F32)
    qs = jnp.concatenate([q2, unsel_rows], axis=1)
    s_parts = [_dot(qs, jnp.concatenate([pg[2 * KV_W:3 * KV_W, :].astype(BF16), selt_ref[p]], axis=0))
               for p, pg in enumerate(pages)]
    k_new = jnp.concatenate([kvn_ref[:, 2 * KV_W:3 * KV_W], zpad], axis=0).astype(BF16)
    v_new = jnp.concatenate([kvn_ref[:, 3 * KV_W:4 * KV_W], zpad], axis=0).astype(BF16)
    qn = jnp.concatenate([qs, onehot], axis=1)
    s_parts.append(_dot_nt(qn, jnp.concatenate([k_new, newt_ref[...]], axis=1)))
    e_s, l_s = _exp2_rows(jnp.concatenate(s_parts, axis=1))
    e_s = e_s.astype(BF16)
    acc = _dot(e_s[:, past:], v_new)
    for p, pg in enumerate(pages):
        acc = acc + _dot_nt(e_s[:, p * page:(p + 1) * page], pg[3 * KV_W:4 * KV_W, :].astype(BF16))
    o_s = acc * (1.0 / jnp.maximum(l_s, TINY))

    wb = wbuf_ref.shape[1]
    kw_new = jnp.concatenate([winn_ref[:, 0:KV_W], zpad], axis=0).astype(BF16)
    vw_new = jnp.concatenate([winn_ref[:, KV_W:], zpad], axis=0).astype(BF16)
    s_w = jnp.concatenate(
        [_dot(qa, jnp.concatenate([wbuf_ref[0:KV_W, :].astype(BF16), wint_ref[...]], axis=0)),
         _dot_nt(qa, jnp.concatenate([kw_new, newt_ref[:, LANES:]], axis=1))], axis=1)
    e_w, l_w = _exp2_rows(s_w)
    e_w = e_w.astype(BF16)
    o_w = (_dot_nt(e_w[:, :wb], wbuf_ref[KV_W:, :].astype(BF16)) + _dot(e_w[:, wb:], vw_new)) \
        * (1.0 / jnp.maximum(l_w, TINY))

    o_ref[...] = _combine_heads(gate_ref[...], o_c, o_s, o_w, nq)


def _attn_sample(page_table, q, gates, kvn, winn, wbuf_t, cache_t, layer, wk, wv, bk, bv, tables, ns):
    db, nq, _ = q.shape
    n_pages = page_table.shape[1]
    page = cache_t.shape[3]
    past = n_pages * page
    wb = wbuf_t.shape[3]
    cmpt, selt, newt, wint, onehot, selmap = tables
    seq3 = lambda s, pt: (s, 0, 0)
    const2 = lambda s, pt: (0, 0)
    const3 = lambda s, pt: (0, 0, 0)

    def page_spec(u, p):
        return pl.BlockSpec((None, None, 4 * KV_W, page), lambda s, pt: (layer, pt[s * ns + u, p], 0, 0))

    grid_spec = pltpu.PrefetchScalarGridSpec(
        num_scalar_prefetch=1,
        grid=(db // ns,),
        in_specs=[pl.BlockSpec((ns, nq, ATTN_W), seq3), pl.BlockSpec((ns, nq, GATE_PAD), seq3),
                  pl.BlockSpec((ns, nq, 4 * KV_W), seq3), pl.BlockSpec((ns, nq, 2 * KV_W), seq3),
                  pl.BlockSpec((None, ns, 2 * KV_W, wb), lambda s, pt: (layer, s, 0, 0))]
                 + [page_spec(u, p) for u in range(ns) for p in range(n_pages)]
                 + [pl.BlockSpec(wk.shape, const3), pl.BlockSpec(wv.shape, const3),
                    pl.BlockSpec((1, KV_W), const2), pl.BlockSpec((1, KV_W), const2),
                    pl.BlockSpec(cmpt.shape, const2), pl.BlockSpec(selt.shape, const3),
                    pl.BlockSpec(newt.shape, const2), pl.BlockSpec(wint.shape, const2),
                    pl.BlockSpec(onehot.shape, const2), pl.BlockSpec(selmap.shape, const2)],
        out_specs=pl.BlockSpec((ns, nq, ATTN_W), seq3),
        scratch_shapes=[pltpu.VMEM((ns, past // CMP_STRIDE + SUBLANES, 2 * KV_W), F32),
                        pltpu.VMEM((ns, past, KV_W), F32), pltpu.VMEM((ns, past, KV_W), F32)],
    )
    return pl.pallas_call(
        functools.partial(_attn_sample_kernel, n_pages=n_pages, past=past, ns=ns),
        grid_spec=grid_spec,
        out_shape=jax.ShapeDtypeStruct((db, nq, ATTN_W), F32),
        compiler_params=_cparams("parallel"),
        name="attn_sample",
    )(page_table, q, gates, kvn, winn, wbuf_t, *([cache_t] * (ns * n_pages)), wk, wv, bk, bv,
      cmpt, selt, newt, wint, onehot, selmap)


def _ln_silu(y, g, b):
    mu = jnp.mean(y, axis=-1, keepdims=True)
    yc = y - mu
    var = jnp.mean(yc * yc, axis=-1, keepdims=True)
    z = yc * lax.rsqrt(var + EPS) * g + b
    return z * _sigmoid(z)


CONV_HALO = 32
CONV_ROWS = 64


def _conv_prompt_kernel(u_ref, prev_ref, w_ref, b_ref, lg_ref, lb_ref, o_ref, ext_ref, *, tiles_per_seq):
    tm = u_ref.shape[0]
    first = (pl.program_id(0) % tiles_per_seq) == 0
    ext_ref[0, 0:CONV_HALO, :] = jnp.where(first, 0.0, prev_ref[...])
    ext_ref[0, CONV_HALO:CONV_HALO + tm, :] = u_ref[...]
    keep = CONV_HALO + tm - SUBLANES
    for b in range(1, SUBLANES):
        ext_ref[b, 0:keep, :] = ext_ref[0, b:b + keep, :]
    off = CONV_HALO - (CONV_WIDTH - 1)

    for r0 in range(0, tm, CONV_ROWS):
        acc = jnp.zeros((CONV_ROWS, u_ref.shape[1]), F32) + b_ref[...]
        for k in range(CONV_WIDTH):
            a, b = divmod(off + k, SUBLANES)
            lo = r0 + a * SUBLANES
            acc = acc + w_ref[k:k + 1, :] * ext_ref[b, lo:lo + CONV_ROWS, :]
        o_ref[r0:r0 + CONV_ROWS, :] = _ln_silu(acc, lg_ref[...], lb_ref[...])


def _conv_prompt(u, w, b, lg, lb, seq, tm):
    n, cc = u.shape
    const = lambda i: (0, 0)
    per_halo = tm // CONV_HALO
    return pl.pallas_call(
        functools.partial(_conv_prompt_kernel, tiles_per_seq=seq // tm),
        grid=(n // tm,),
        in_specs=[pl.BlockSpec((tm, cc), lambda i: (i, 0)),
                  pl.BlockSpec((CONV_HALO, cc), lambda i: (jnp.maximum(i * per_halo - 1, 0), 0)),
                  pl.BlockSpec(w.shape, const), pl.BlockSpec((1, cc), const),
                  pl.BlockSpec((1, cc), const), pl.BlockSpec((1, cc), const)],
        out_specs=pl.BlockSpec((tm, cc), lambda i: (i, 0)),
        out_shape=jax.ShapeDtypeStruct((n, cc), F32),
        scratch_shapes=[pltpu.VMEM((SUBLANES, CONV_HALO + tm, cc), F32)],
        compiler_params=_cparams("parallel"),
        name="conv_prompt",
    )(u, u, w, b, lg, lb)


def _conv_sample_kernel(ext_ref, w_ref, b_ref, lg_ref, lb_ref, o_ref):
    ns, nq, cc = o_ref.shape
    for s in range(ns):
        acc = jnp.zeros((nq, cc), F32) + b_ref[...]
        for k in range(CONV_WIDTH):
            acc = acc + w_ref[k:k + 1, :] * ext_ref[s, pl.ds(k, nq), :]
        o_ref[s] = _ln_silu(acc, lg_ref[...], lb_ref[...])


def _conv_sample(ext, w, b, lg, lb, nq, ns):
    db, rows, cc = ext.shape
    const = lambda i: (0, 0)
    return pl.pallas_call(
        _conv_sample_kernel,
        grid=(db // ns,),
        in_specs=[pl.BlockSpec((ns, rows, cc), lambda i: (i, 0, 0)),
                  pl.BlockSpec(w.shape, const), pl.BlockSpec((1, cc), const),
                  pl.BlockSpec((1, cc), const), pl.BlockSpec((1, cc), const)],
        out_specs=pl.BlockSpec((ns, nq, cc), lambda i: (i, 0, 0)),
        out_shape=jax.ShapeDtypeStruct((db, nq, cc), F32),
        compiler_params=_cparams("parallel"),
        name="conv_sample",
    )(ext, w, b, lg, lb)


def _merge_kernel(x_ref, gpre_ref, oa_ref, oc_ref, wm_ref, wao_ref, wco_ref, wout_ref, gpost_ref, xo_ref):
    x = x_ref[...]
    d = x.shape[1]
    zb = _rms(x, gpre_ref[...]).astype(BF16)
    gm = _sigmoid(_dot(zb, wm_ref[...]))
    a = _dot(oa_ref[...].astype(BF16), wao_ref[...])
    c = _dot(oc_ref[...].astype(BF16), wco_ref[...])
    mix = gm[:, :d] * a + gm[:, d:] * c
    h = _dot(mix.astype(BF16), wout_ref[...])
    xo_ref[...] = x + _rms(h, gpost_ref[...])


def _merge(x, gpre, oa, oc, wm, wao, wco, wout, gpost, tm):
    n, d = x.shape
    const = lambda i: (0, 0)
    row = lambda i: (i, 0)
    return pl.pallas_call(
        _merge_kernel,
        grid=(n // tm,),
        in_specs=[pl.BlockSpec((tm, d), row), pl.BlockSpec((1, d), const),
                  pl.BlockSpec((tm, oa.shape[1]), row), pl.BlockSpec((tm, oc.shape[1]), row),
                  pl.BlockSpec(wm.shape, const), pl.BlockSpec(wao.shape, const),
                  pl.BlockSpec(wco.shape, const), pl.BlockSpec(wout.shape, const),
                  pl.BlockSpec((1, d), const)],
        out_specs=pl.BlockSpec((tm, d), row),
        out_shape=jax.ShapeDtypeStruct((n, d), F32),
        compiler_params=_cparams("parallel"),
        name="merge",
    )(x, gpre, oa, oc, wm, wao, wco, wout, gpost)


FFN_HALO = SUBLANES


def _ffn_kernel(*refs, sample, tiles_per_seq, tf, nq):
    if sample:
        (x_ref, gpre_ref, wg_ref, wv_ref, cw_ref, cb_ref, wd_ref, gpost_ref, h1_ref, h2_ref,
         xo_ref, up_ref, ext_ref) = refs
    else:
        (x_ref, gpre_ref, wg_ref, wv_ref, cw_ref, cb_ref, wd_ref, gpost_ref,
         xo_ref, up_ref, ext_ref, carry_ref) = refs
    x = x_ref[...]
    tm, d = x.shape
    dff = wd_ref.shape[0]
    zb = _rms(x, gpre_ref[...]).astype(BF16)
    first = (pl.program_id(0) % tiles_per_seq) == 0
    tpos = lax.broadcasted_iota(jnp.int32, (tm, 1), 0) & (nq - 1)
    if not sample:
        @pl.when(pl.program_id(0) == 0)
        def _():
            carry_ref[...] = jnp.zeros(carry_ref.shape, F32)

    def conv_half(w_ref, c0, col0):
        up = _dot(zb, w_ref[:, c0:c0 + tf])
        ext_ref[FFN_HALO:FFN_HALO + tm, :] = up
        if sample:
            ext_ref[0:FFN_HALO, :] = jnp.zeros((FFN_HALO, tf), F32)
            up_ref[:, col0:col0 + tf] = up
        else:
            ext_ref[0:FFN_HALO, :] = jnp.where(first, 0.0, carry_ref[:, col0:col0 + tf])
            carry_ref[:, col0:col0 + tf] = up[tm - FFN_HALO:, :]
            up_ref[:, col0:col0 + tf] = up[tm - FFN_HALO:, :]
        s1 = ext_ref[pl.ds(FFN_HALO - 1, tm), :]
        s2 = ext_ref[pl.ds(FFN_HALO - 2, tm), :]
        if sample:
            s1 = jnp.where(tpos == 0, h1_ref[:, col0:col0 + tf], s1)
            s2 = jnp.where(tpos <= 1, h2_ref[:, col0:col0 + tf], s2)
        cw = cw_ref[:, col0:col0 + tf]
        return cw[0:1] * s2 + cw[1:2] * s1 + cw[2:3] * up + cb_ref[:, col0:col0 + tf]

    acc = jnp.zeros((tm, d), F32)
    for c in range(dff // tf):
        hg = conv_half(wg_ref, c * tf, c * tf)
        hv = conv_half(wv_ref, c * tf, dff + c * tf)
        act = (_gelu_tanh(hg) * hv).astype(BF16)
        acc = acc + _dot(act, wd_ref[c * tf:(c + 1) * tf, :])
    xo_ref[...] = x + _rms(acc, gpost_ref[...])


def _ffn(x, gpre, wg, wv, cw, cb, wd, gpost, tm, tf, seq=None, hist=None, nq=None):
    n, d = x.shape
    dff = wd.shape[0]
    sample = hist is not None
    const = lambda i: (0, 0)
    row = lambda i: (i, 0)
    single = pl.Buffered(1)
    in_specs = [pl.BlockSpec((tm, d), row), pl.BlockSpec((1, d), const),
                pl.BlockSpec(wg.shape, const, pipeline_mode=single),
                pl.BlockSpec(wv.shape, const, pipeline_mode=single),
                pl.BlockSpec(cw.shape, const), pl.BlockSpec(cb.shape, const),
                pl.BlockSpec(wd.shape, const, pipeline_mode=single), pl.BlockSpec((1, d), const)]
    args = [x, gpre, wg, wv, cw, cb, wd, gpost]
    scratch = [pltpu.VMEM((FFN_HALO + tm, tf), F32)]
    if sample:
        in_specs += [pl.BlockSpec((tm, 2 * dff), row)] * 2
        args += list(hist)
        up_rows, tiles_per_seq = tm, 1
    else:
        scratch.append(pltpu.VMEM((FFN_HALO, 2 * dff), F32))
        up_rows, tiles_per_seq, nq = FFN_HALO, seq // tm, tm
    return pl.pallas_call(
        functools.partial(_ffn_kernel, sample=sample, tiles_per_seq=tiles_per_seq, tf=tf, nq=nq),
        grid=(n // tm,),
        in_specs=in_specs,
        out_specs=[pl.BlockSpec((tm, d), row), pl.BlockSpec((up_rows, 2 * dff), row)],
        out_shape=[jax.ShapeDtypeStruct((n, d), F32),
                   jax.ShapeDtypeStruct((n // tm * up_rows, 2 * dff), F32)],
        scratch_shapes=scratch,
        compiler_params=_cparams("arbitrary"),
        name="ffn_sample" if sample else "ffn_prompt",
    )(*args)


def _bias(cond):
    return jnp.asarray(np.where(cond, NEG_BIG, 0.0).astype(np.float32), dtype=BF16)


def _onehot_rows(rows, nq):
    return jnp.asarray((np.arange(LANES)[None, :] == (np.arange(rows)[:, None] % nq)).astype(np.float32), dtype=BF16)


def _cmp_sel_table(n_rows, n_cmp, n_slc):
    s = np.arange(n_rows) * CMP_STRIDE
    e = s + CMP_LEN - 1
    js = np.arange(LANES) * SEL_LEN
    je = js + SEL_LEN - 1
    m = (s[:, None] <= je[None, :]) & (e[:, None] >= js[None, :])
    m &= (np.arange(n_rows)[:, None] < n_cmp) & (np.arange(LANES)[None, :] < n_slc)
    return m.astype(np.float32)


def _prompt_tables(seq, tk):
    nq = Q_BLOCK
    n_rows = seq // CMP_STRIDE
    lane = np.arange(LANES)
    n = np.arange(n_rows)
    blk = np.arange(seq // nq)
    cmpt = _bias(CMP_STRIDE * n[None, :, None] + CMP_LEN - 1 > nq * blk[:, None, None] + lane[None, None, :])
    key = np.arange(seq).reshape(seq // tk, tk)
    selt = _bias(key[:, :, None] // SEL_LEN == (lane % HEAD_DIM)[None, None, :])
    i = np.arange(tk)
    d = np.arange(tk // nq + 1)
    cast = _bias((i[None, :, None] > nq * d[:, None, None] + lane[None, None, :]) & (d < tk // nq)[:, None, None])
    span = np.arange(WINDOW + nq)
    w = np.arange(WINDOW // nq + 1)
    dt = (nq * w[:, None, None] + lane[None, None, :]) - span[None, :, None]
    wint = _bias((dt < 0) | (dt > WINDOW))
    n_cmp = (seq - CMP_LEN) // CMP_STRIDE + 1
    selmap_t = jnp.asarray(_cmp_sel_table(n_rows, n_cmp, seq // SEL_LEN).T, dtype=BF16)
    return cmpt, selt, cast, wint, _onehot_rows(N_HEADS * nq, nq), selmap_t


def _sample_tables(past, page, nq, wb):
    n_rows = past // CMP_STRIDE
    lane = np.arange(LANES)
    n = np.arange(n_rows)
    qvalid = lane < nq
    cmpt = _bias((CMP_STRIDE * n[:, None] + CMP_LEN - 1 > past + lane[None, :]) & qvalid[None, :])
    key = np.arange(past).reshape(past // page, 1, page)
    selt = _bias(key // SEL_LEN == lane[None, :, None])
    i = np.arange(page)
    new_sel = (past + i[:, None]) // SEL_LEN == lane[None, :]
    new_causal = (i[:, None] > lane[None, :]) & qvalid[None, :]
    newt = _bias(np.concatenate([new_sel, new_causal], axis=1))
    dt = (lane[:, None] + wb) - np.arange(wb)[None, :]
    wint = _bias(((dt < 0) | (dt > WINDOW)) & qvalid[:, None])
    n_cmp = (past + nq - CMP_LEN) // CMP_STRIDE + 1
    selmap = jnp.asarray(_cmp_sel_table(n_rows, n_cmp, -(-(past + nq) // SEL_LEN)), dtype=BF16)
    return cmpt, selt, newt, wint, _onehot_rows(N_HEADS * nq, nq), selmap


def _phi_weights(w):
    depth = w.shape[0]
    wj = w.reshape(depth, CMP_LEN, HEAD_DIM, HEAD_DIM)
    eye = jnp.eye(N_KV, dtype=w.dtype)
    bd = jnp.einsum("gh,ljde->ljgdhe", eye, wj).reshape(depth, CMP_LEN, KV_W, KV_W)
    return jnp.concatenate([bd[:, :CMP_STRIDE], bd[:, CMP_STRIDE:]], axis=-1).astype(BF16)


def kernel(x_prompt, x_sample, cache_kv, state_kv_win, state_conv, state_ffn, page_table, norm_mix_pre,
           norm_mix_post, norm_ffn_pre, norm_ffn_post, w_in, w_phi_k, b_phi_k, w_phi_v, b_phi_v, w_attn_out,
           conv_w, conv_b, conv_ln_g, conv_ln_b, w_conv_out, w_out, w_up, ffn_conv_w, ffn_conv_b, w_down):
    batch, seq, d = x_prompt.shape
    db, nq, _ = x_sample.shape
    depth, n_phys, page = cache_kv.shape[:3]
    n_pages = page_table.shape[1]
    past = n_pages * page
    wb = state_kv_win.shape[2]
    cc = conv_w.shape[2]
    dff = w_down.shape[1]
    assert seq % Q_BLOCK == 0 and nq == SUBLANES and page == LANES and wb == WINDOW

    tm = 512
    tm_s = 128
    tf = 256
    tk = 512
    ns_attn = 2
    assert dff % tf == 0 and seq % tk == 0 and tk % Q_BLOCK == 0 and WINDOW % Q_BLOCK == 0

    c1 = ATTN_W + 6 * KV_W
    c2 = c1 + 3 * N_HEADS
    c3 = c2 + 2 * cc
    wq = w_in[:, :, :c1].astype(BF16)
    wgate = jnp.pad(w_in[:, :, c1:c2], ((0, 0), (0, 0), (0, GATE_PAD - 3 * N_HEADS))).astype(BF16)
    wu = w_in[:, :, c2:c3].astype(BF16)
    wm = w_in[:, :, c3:].astype(BF16)
    wao = w_attn_out.astype(BF16)
    wco = w_conv_out.astype(BF16)
    wout = w_out.astype(BF16)
    wup_g = w_up[:, :, :dff].astype(BF16)
    wup_v = w_up[:, :, dff:].astype(BF16)
    wdn = w_down.astype(BF16)
    wk_ab = _phi_weights(w_phi_k)
    wv_ab = _phi_weights(w_phi_v)
    bk2 = jnp.tile(b_phi_k, (1, N_KV)).reshape(depth, 1, KV_W)
    bv2 = jnp.tile(b_phi_v, (1, N_KV)).reshape(depth, 1, KV_W)

    tables_p = _prompt_tables(seq, tk)
    tables_s = _sample_tables(past, page, nq, wb)

    cache_t = cache_kv.transpose(0, 1, 3, 4, 5, 2).reshape(depth, n_phys, 4 * KV_W, page)
    wbuf_t = state_kv_win.transpose(0, 1, 3, 4, 5, 2).reshape(depth, db, 2 * KV_W, wb)
    row = lambda v: v.reshape(1, -1)

    xp = x_prompt.reshape(batch * seq, d)
    xs = x_sample.reshape(db * nq, d)
    outs = [[] for _ in range(8)]
    for l in range(depth):
        gpre, gpost = row(norm_mix_pre[l]), row(norm_mix_post[l])
        fpre, fpost = row(norm_ffn_pre[l]), row(norm_ffn_post[l])
        cb, lg, lb = row(conv_b[l]), row(conv_ln_g[l]), row(conv_ln_b[l])
        fcb = row(ffn_conv_b[l])

        q, kv, win, att_b, gates, u = _in_proj(xp, gpre, wq[l], wgate[l], wu[l], tm)
        kcmp, vcmp = _compress_prompt(kv, wk_ab[l], wv_ab[l], bk2[l], bv2[l], batch, seq)
        oa = _attn_prompt(q, gates, att_b, kcmp, vcmp, tables_p, batch, seq, tk)
        oc = _conv_prompt(u, conv_w[l], cb, lg, lb, seq, tm)
        xp = _merge(xp, gpre, oa, oc, wm[l], wao[l], wco[l], wout[l], gpost, tm)
        xp, up_tail = _ffn(xp, fpre, wup_g[l], wup_v[l], ffn_conv_w[l], fcb, wdn[l], fpost, tm, tf, seq=seq)
        outs[0].append(kv.reshape(batch, seq, 4, N_KV, HEAD_DIM))
        outs[2].append(win.reshape(batch, seq, 2, N_KV, HEAD_DIM)[:, seq - min(WINDOW, seq):])
        outs[4].append(u.reshape(batch, seq, cc)[:, seq - (CONV_WIDTH - 1):])
        outs[6].append(up_tail.reshape(batch, seq // tm, FFN_HALO, 2 * dff)[:, -1, FFN_HALO - (FFN_CONV_WIDTH - 1):])

        q, kv, win, _, gates, u = _in_proj(xs, gpre, wq[l], wgate[l], wu[l], tm)
        oa = _attn_sample(page_table, q.reshape(db, nq, ATTN_W), gates.reshape(db, nq, GATE_PAD),
                          kv.reshape(db, nq, 4 * KV_W), win.reshape(db, nq, 2 * KV_W), wbuf_t, cache_t, l,
                          wk_ab[l], wv_ab[l], bk2[l], bv2[l], tables_s, ns_attn)
        ext = jnp.concatenate([state_conv[l], u.reshape(db, nq, cc)], axis=1)
        oc = _conv_sample(ext, conv_w[l], cb, lg, lb, nq, 16)
        xs = _merge(xs, gpre, oa.reshape(db * nq, ATTN_W), oc.reshape(db * nq, cc),
                    wm[l], wao[l], wco[l], wout[l], gpost, tm)
        hist = state_ffn[l]
        h2 = jnp.pad(hist, ((0, 0), (0, nq - (FFN_CONV_WIDTH - 1)), (0, 0))).reshape(db * nq, 2 * dff)
        h1 = jnp.pad(hist[:, 1:], ((0, 0), (0, nq - 1), (0, 0))).reshape(db * nq, 2 * dff)
        xs, up_s = _ffn(xs, fpre, wup_g[l], wup_v[l], ffn_conv_w[l], fcb, wdn[l], fpost, tm_s, tf,
                        hist=(h1, h2), nq=nq)
        outs[1].append(kv.reshape(db, nq, 4, N_KV, HEAD_DIM))
        win_all = jnp.concatenate([state_kv_win[l], win.reshape(db, nq, 2, N_KV, HEAD_DIM)], axis=1)
        outs[3].append(win_all[:, win_all.shape[1] - WINDOW:])
        outs[5].append(ext[:, nq:])
        outs[7].append(up_s.reshape(db, nq, 2 * dff)[:, nq - (FFN_CONV_WIDTH - 1):])

    return (xp.reshape(batch, seq, d), xs.reshape(db, nq, d),
            jnp.stack(outs[0]), jnp.stack(outs[1]), jnp.stack(outs[2]), jnp.stack(outs[3]),
            jnp.stack(outs[4]), jnp.stack(outs[5]), jnp.stack(outs[6]), jnp.stack(outs[7]))
```

```python
import functools

import numpy as np
import jax
import jax.numpy as jnp
from jax import lax
from jax.experimental import pallas as pl
from jax.experimental.pallas import tpu as pltpu

F32 = jnp.float32
BF16 = jnp.bfloat16

N_HEADS = 8
N_KV = 2
GROUP = N_HEADS // N_KV
HEAD_DIM = 64
ATTN_W = N_HEADS * HEAD_DIM
KV_W = N_KV * HEAD_DIM
CMP_LEN = 32
CMP_STRIDE = 16
SEL_LEN = 64
N_SEL = 8
WINDOW = 512
Q_BLOCK = 128
FORCE_BONUS = 1000.0
CONV_WIDTH = 31
FFN_CONV_WIDTH = 3
EPS = 1e-6
TINY = 1e-30
NEG_BIG = -(2.0 ** 100)
LOG2E = 1.4426950408889634
GATE_PAD = 128

LANES = 128
SUBLANES = 8
VMEM_LIMIT = 56 * 1024 * 1024


def _cparams(*sem):
    return pltpu.CompilerParams(dimension_semantics=sem, vmem_limit_bytes=VMEM_LIMIT)


def _rms(x, g):
    return x * lax.rsqrt(jnp.mean(x * x, axis=-1, keepdims=True) + EPS) * g


def _sigmoid(x):
    return 1.0 / (1.0 + jnp.exp(-x))


def _gelu_tanh(x):
    c = np.sqrt(2.0 / np.pi).astype(np.float32)
    return x * (0.5 * (1.0 + jnp.tanh(c * (x + 0.044715 * (x * x * x)))))


def _dot(a, b):
    return jnp.dot(a, b, preferred_element_type=F32)


def _dot_nt(a, b):
    return lax.dot_general(a, b, (((1,), (1,)), ((), ())), preferred_element_type=F32)


def _exp2_rows(s):
    m = jnp.max(s, axis=-1, keepdims=True)
    e = jnp.exp2(s - m)
    return e, jnp.sum(e, axis=-1, keepdims=True)


def _build_q2(q, nq):
    lane = lax.broadcasted_iota(jnp.int32, (nq, LANES), 1)
    parts = []
    for h in range(N_HEADS):
        g = h // GROUP
        c = h // 2
        chunk = q[:, c * LANES:(c + 1) * LANES]
        if (h % 2) != g:
            chunk = pltpu.roll(chunk, HEAD_DIM, 1)
        keep = (lane >= g * HEAD_DIM) & (lane < (g + 1) * HEAD_DIM)
        parts.append(jnp.where(keep, chunk * (HEAD_DIM ** -0.5 * LOG2E), 0.0))
    return jnp.concatenate(parts, axis=0).astype(BF16)


def _row_positions(nq, t0):
    r = lax.broadcasted_iota(jnp.int32, (N_HEADS * nq, 1), 0)
    return t0 + (r & (nq - 1))


def _unselected_blocks_t(imp_t, t_lane, n_slc):
    jb = lax.broadcasted_iota(jnp.int32, imp_t.shape, 0)
    cur = t_lane >> 6
    valid = (jb * SEL_LEN) <= t_lane
    forced = (jb == 0) | (jb == cur) | (jb == cur - 1)
    score = jnp.where(valid, imp_t + FORCE_BONUS * forced.astype(F32), -1.0)
    score = jnp.where(jb < n_slc, score, -2.0)
    jbf = jb.astype(F32)
    unsel = jnp.ones(imp_t.shape, F32)
    for _ in range(N_SEL):
        m = jnp.max(score, axis=0, keepdims=True)
        idx = jnp.min(jnp.where(score == m, jbf, 1e9), axis=0, keepdims=True)
        hit = jbf == idx
        unsel = jnp.where(hit, 0.0, unsel)
        score = jnp.where(hit, -3.0, score)
    return unsel


def _split_dot(w, x):
    hi = x.astype(BF16)
    lo = (x - hi.astype(F32)).astype(BF16)
    return _dot(w, hi) + _dot(w, lo)


def _combine_heads(gates, o_c, o_s, o_w, nq):
    lane = lax.broadcasted_iota(jnp.int32, (nq, LANES), 1)
    heads = []
    for h in range(N_HEADS):
        sl = slice(h * nq, (h + 1) * nq)
        o = (gates[:, h:h + 1] * o_c[sl]
             + gates[:, N_HEADS + h:N_HEADS + h + 1] * o_s[sl]
             + gates[:, 2 * N_HEADS + h:2 * N_HEADS + h + 1] * o_w[sl])
        if (h % 2) != (h // GROUP):
            o = pltpu.roll(o, HEAD_DIM, 1)
        heads.append(o)
    chunks = [jnp.where(lane < HEAD_DIM, heads[2 * c], heads[2 * c + 1]) for c in range(N_HEADS // 2)]
    return jnp.concatenate(chunks, axis=1)


def _in_proj_kernel(x_ref, g_ref, wq_ref, wg_ref, wu_ref, q_ref, kv_ref, win_ref, att_ref, gate_ref, u_ref):
    zb = _rms(x_ref[...], g_ref[...]).astype(BF16)
    qkv = _dot(zb, wq_ref[...])
    q_ref[...] = qkv[:, :ATTN_W]
    kv_ref[...] = qkv[:, ATTN_W:ATTN_W + 4 * KV_W]
    win_ref[...] = qkv[:, ATTN_W + 4 * KV_W:]
    att_ref[...] = qkv[:, ATTN_W + 2 * KV_W:].astype(BF16)
    gate_ref[...] = _sigmoid(_dot(zb, wg_ref[...]))
    u2 = _dot(zb, wu_ref[...])
    cc = u2.shape[1] // 2
    u_ref[...] = u2[:, :cc] * _sigmoid(u2[:, cc:])


def _in_proj(x, g, wq, wg, wu, tm):
    n, d = x.shape
    cc = wu.shape[1] // 2
    const = lambda i: (0, 0)
    row = lambda i: (i, 0)
    return pl.pallas_call(
        _in_proj_kernel,
        grid=(n // tm,),
        in_specs=[pl.BlockSpec((tm, d), row), pl.BlockSpec((1, d), const),
                  pl.BlockSpec(wq.shape, const), pl.BlockSpec(wg.shape, const), pl.BlockSpec(wu.shape, const)],
        out_specs=[pl.BlockSpec((tm, ATTN_W), row), pl.BlockSpec((tm, 4 * KV_W), row),
                   pl.BlockSpec((tm, 2 * KV_W), row), pl.BlockSpec((tm, 4 * KV_W), row),
                   pl.BlockSpec((tm, GATE_PAD), row),
                   pl.BlockSpec((tm, cc), row)],
        out_shape=[jax.ShapeDtypeStruct((n, ATTN_W), F32), jax.ShapeDtypeStruct((n, 4 * KV_W), F32),
                   jax.ShapeDtypeStruct((n, 2 * KV_W), F32), jax.ShapeDtypeStruct((n, 4 * KV_W), BF16),
                   jax.ShapeDtypeStruct((n, GATE_PAD), F32),
                   jax.ShapeDtypeStruct((n, cc), F32)],
        compiler_params=_cparams("parallel"),
        name="in_proj",
    )(x, g, wq, wg, wu)


def _compress_prompt_kernel(k_ref, v_ref, wk_ref, wv_ref, bk_ref, bv_ref, kc_ref, vc_ref, sh_ref, *, n_rows):
    acc_k = jnp.zeros((n_rows, 2 * KV_W), F32)
    acc_v = jnp.zeros((n_rows, 2 * KV_W), F32)
    for j in range(CMP_STRIDE):
        rk = k_ref[pl.ds(j, n_rows, stride=CMP_STRIDE), :].astype(BF16)
        rv = v_ref[pl.ds(j, n_rows, stride=CMP_STRIDE), :].astype(BF16)
        acc_k = acc_k + _dot(rk, wk_ref[j])
        acc_v = acc_v + _dot(rv, wv_ref[j])
    sh_ref[0:n_rows, 0:KV_W] = acc_k[:, KV_W:]
    sh_ref[0:n_rows, KV_W:] = acc_v[:, KV_W:]
    sh_ref[n_rows:n_rows + SUBLANES, :] = jnp.zeros((SUBLANES, 2 * KV_W), F32)
    nxt = sh_ref[pl.ds(1, n_rows), :]
    kc_ref[...] = acc_k[:, :KV_W] + nxt[:, :KV_W] + bk_ref[...]
    vc_ref[...] = acc_v[:, :KV_W] + nxt[:, KV_W:] + bv_ref[...]


def _compress_prompt(kv, wk, wv, bk, bv, batch, seq):
    n_rows = seq // CMP_STRIDE
    const2 = lambda b: (0, 0)
    const3 = lambda b: (0, 0, 0)
    return pl.pallas_call(
        functools.partial(_compress_prompt_kernel, n_rows=n_rows),
        grid=(batch,),
        in_specs=[pl.BlockSpec((seq, KV_W), lambda b: (b, 0)), pl.BlockSpec((seq, KV_W), lambda b: (b, 1)),
                  pl.BlockSpec(wk.shape, const3), pl.BlockSpec(wv.shape, const3),
                  pl.BlockSpec((1, KV_W), const2), pl.BlockSpec((1, KV_W), const2)],
        out_specs=[pl.BlockSpec((n_rows, KV_W), lambda b: (b, 0))] * 2,
        out_shape=[jax.ShapeDtypeStruct((batch * n_rows, KV_W), F32)] * 2,
        scratch_shapes=[pltpu.VMEM((n_rows + SUBLANES, 2 * KV_W), F32)],
        compiler_params=_cparams("parallel"),
        name="compress_prompt",
    )(kv, kv, wk, wv, bk, bv)


def _softmax_pv(s, v_aug):
    m = jnp.max(s, axis=-1, keepdims=True)
    return _dot(jnp.exp2(s - m), v_aug)


def _group_values(v):
    lane = lax.broadcasted_iota(jnp.int32, v.shape, 1)
    one = jnp.ones(v.shape, v.dtype)
    return jnp.where(lane < HEAD_DIM, v, one), jnp.where(lane >= HEAD_DIM, v, one)


def _normalise(acc):
    return acc * (1.0 / jnp.maximum(pltpu.roll(acc, HEAD_DIM, 1), TINY))


def _attn_prompt_kernel(q_ref, gate_ref, ksv_ref, kwv_ref, kc_ref, vc_ref, cmpt_ref, selt_ref, cast_ref,
                        wint_ref, onehot_ref, selmap_ref, o_ref, *, seq, tk):
    nq = Q_BLOCK
    rows = N_HEADS * nq
    half = rows // N_KV
    iblk = pl.program_id(1)
    start = iblk * nq
    q2 = _build_q2(q_ref[...], nq)
    onehot = onehot_ref[...]
    qa = jnp.concatenate([q2, onehot], axis=1)
    t_rows = _row_positions(nq, start)

    def two_groups(s, v):
        v0, v1 = _group_values(v)
        return jnp.concatenate([_softmax_pv(s[:half], v0), _softmax_pv(s[half:], v1)], axis=0)

    kca = jnp.concatenate([kc_ref[...].astype(BF16), cmpt_ref[...]], axis=1)
    s_c = _dot_nt(qa, kca).astype(BF16)
    o_c = jnp.where(t_rows >= CMP_LEN - 1, _normalise(two_groups(s_c, vc_ref[...].astype(BF16))), 0.0)

    s_t = _dot_nt(kca, qa)
    e_t = jnp.exp2(s_t - jnp.max(s_t, axis=0, keepdims=True))
    l_t = jnp.sum(e_t, axis=0, keepdims=True)
    t_lane = start + (lax.broadcasted_iota(jnp.int32, (1, rows), 1) & (nq - 1))
    p_t = e_t * jnp.where(t_lane >= CMP_LEN - 1, 1.0 / jnp.maximum(l_t, TINY), 0.0)
    ps_t = []
    for g in range(N_KV):
        acc = p_t[:, (g * GROUP) * nq:(g * GROUP + 1) * nq]
        for r in range(1, GROUP):
            acc = acc + p_t[:, (g * GROUP + r) * nq:(g * GROUP + r + 1) * nq]
        ps_t.append(acc)
    ps_t = jnp.concatenate(ps_t, axis=1)
    imp_t = _split_dot(selmap_ref[...], ps_t)
    unsel = _unselected_blocks_t(imp_t, t_lane[:, :N_KV * nq], seq // SEL_LEN).T

    lane_q = lax.broadcasted_iota(jnp.int32, (half, LANES), 1)
    lane_k = lax.broadcasted_iota(jnp.int32, (tk, LANES), 1)
    bits0 = pltpu.roll(unsel[:nq], HEAD_DIM, 1).astype(BF16)
    bits1 = unsel[nq:].astype(BF16)
    lhs = (jnp.where(lane_q < HEAD_DIM, q2[:half], jnp.concatenate([bits0] * GROUP, axis=0)),
           jnp.where(lane_q >= HEAD_DIM, q2[half:], jnp.concatenate([bits1] * GROUP, axis=0)))

    lhs = tuple(jnp.concatenate([x, onehot[:half]], axis=1) for x in lhs)
    n_full = start // tk

    def scores(j):
        k0 = pl.multiple_of(j * tk, tk)
        ks = ksv_ref[pl.ds(k0, tk), 0:KV_W]
        tab = selt_ref[j]
        causal = cast_ref[jnp.where(j == n_full, iblk - n_full * (tk // nq), tk // nq)]
        rhs = (jnp.concatenate([jnp.where(lane_k < HEAD_DIM, ks, tab), causal], axis=1),
               jnp.concatenate([jnp.where(lane_k >= HEAD_DIM, ks, tab), causal], axis=1))
        return tuple(_dot_nt(lhs[g], rhs[g]).astype(BF16) for g in range(N_KV))

    def consume(j, s, carry):
        k0 = pl.multiple_of(j * tk, tk)
        vals = _group_values(ksv_ref[pl.ds(k0, tk), KV_W:2 * KV_W])
        out = []
        for g in range(N_KV):
            m, acc = carry[g]
            m_new = jnp.maximum(m, jnp.max(s[g], axis=-1, keepdims=True).astype(F32))
            p = jnp.exp2(s[g] - m_new.astype(BF16))
            out.append((m_new, jnp.exp2(m - m_new) * acc + _dot(p, vals[g])))
        return tuple(out)

    def body(j, state):
        s, carry = state
        return scores(j + 1), consume(j, s, carry)

    init = tuple((jnp.full((half, 1), -3.0e38, F32), jnp.zeros((half, KV_W), F32)) for _ in range(N_KV))
    s_last, carry = lax.fori_loop(0, n_full, body, (scores(0), init))
    carry = consume(n_full, s_last, carry)
    o_s = _normalise(jnp.concatenate([carry[g][1] for g in range(N_KV)], axis=0))

    span = WINDOW + nq
    ws = pl.multiple_of(jnp.maximum(start - WINDOW, 0), nq)
    kt = jnp.concatenate([kwv_ref[pl.ds(ws, span), 0:KV_W], wint_ref[jnp.minimum(iblk, WINDOW // nq)]], axis=1)
    s_w = _dot_nt(qa, kt).astype(BF16)
    o_w = _normalise(two_groups(s_w, kwv_ref[pl.ds(ws, span), KV_W:2 * KV_W]))

    o_ref[...] = _combine_heads(gate_ref[...], o_c, o_s, o_w, nq)


def _attn_prompt(q, gates, att_b, kcmp, vcmp, tables, batch, seq, tk):
    nb = seq // Q_BLOCK
    n_cmp = kcmp.shape[0] // batch
    cmpt, selt, cast, wint, onehot, selmap_t = tables
    blk = lambda b, i: (b * nb + i, 0)
    per_b = lambda b, i: (b, 0)
    const = lambda b, i: (0, 0)
    const3 = lambda b, i: (0, 0, 0)
    return pl.pallas_call(
        functools.partial(_attn_prompt_kernel, seq=seq, tk=tk),
        grid=(batch, nb),
        in_specs=[pl.BlockSpec((Q_BLOCK, ATTN_W), blk), pl.BlockSpec((Q_BLOCK, GATE_PAD), blk),
                  pl.BlockSpec((seq, 2 * KV_W), lambda b, i: (b, 0)),
                  pl.BlockSpec((seq, 2 * KV_W), lambda b, i: (b, 1)),
                  pl.BlockSpec((n_cmp, KV_W), per_b), pl.BlockSpec((n_cmp, KV_W), per_b),
                  pl.BlockSpec((None, n_cmp, LANES), lambda b, i: (i, 0, 0)),
                  pl.BlockSpec(selt.shape, const3), pl.BlockSpec(cast.shape, const3),
                  pl.BlockSpec(wint.shape, const3), pl.BlockSpec(onehot.shape, const),
                  pl.BlockSpec(selmap_t.shape, const)],
        out_specs=pl.BlockSpec((Q_BLOCK, ATTN_W), blk),
        out_shape=jax.ShapeDtypeStruct((batch * seq, ATTN_W), F32),
        compiler_params=_cparams("parallel", "parallel"),
        name="attn_prompt",
    )(q, gates, att_b, att_b, kcmp, vcmp, cmpt, selt, cast, wint, onehot, selmap_t)


def _attn_sample_kernel(pt_ref, q_ref, gate_ref, kvn_ref, winn_ref, wbuf_ref, *rest, n_pages, past, ns):
    del pt_ref
    consts = rest[ns * n_pages:ns * n_pages + 10]
    o_ref, wout_ref, sh_ref, kraw_ref, vraw_ref = rest[ns * n_pages + 11:]
    for u in range(ns):
        _attn_sample_one(q_ref.at[u], gate_ref.at[u], kvn_ref.at[u], winn_ref.at[u], wbuf_ref.at[u],
                         rest[u * n_pages:(u + 1) * n_pages], consts, o_ref.at[u], wout_ref.at[u], sh_ref.at[u],
                         kraw_ref.at[u], vraw_ref.at[u], past)


def _attn_sample_one(q_ref, gate_ref, kvn_ref, winn_ref, wbuf_ref, pages, consts, o_ref, wout_ref, sh_ref,
                     kraw_ref, vraw_ref, past):
    wk_ref, wv_ref, bk_ref, bv_ref, cmpt_ref, selt_ref, newt_ref, wint_ref, onehot_ref, selmap_ref = consts
    nq = q_ref.shape[0]
    page = pages[0].shape[1]
    n_rows = past // CMP_STRIDE
    q2 = _build_q2(q_ref[...], nq)
    onehot = onehot_ref[...]
    qa = jnp.concatenate([q2, onehot], axis=1)
    t_rows = _row_positions(nq, past)

    for p, pg in enumerate(pages):
        kraw_ref[p * page:(p + 1) * page, :] = pg[0:KV_W, :].T
        vraw_ref[p * page:(p + 1) * page, :] = pg[KV_W:2 * KV_W, :].T
    acc_k = jnp.zeros((n_rows, 2 * KV_W), F32)
    acc_v = jnp.zeros((n_rows, 2 * KV_W), F32)
    for j in range(CMP_STRIDE):
        rk = kraw_ref[pl.ds(j, n_rows, stride=CMP_STRIDE), :]
        rv = vraw_ref[pl.ds(j, n_rows, stride=CMP_STRIDE), :]
        acc_k = acc_k + _dot(rk.astype(BF16), wk_ref[j])
        acc_v = acc_v + _dot(rv.astype(BF16), wv_ref[j])
    sh_ref[0:n_rows, 0:KV_W] = acc_k[:, KV_W:]
    sh_ref[0:n_rows, KV_W:] = acc_v[:, KV_W:]
    sh_ref[n_rows:n_rows + SUBLANES, :] = jnp.zeros((SUBLANES, 2 * KV_W), F32)
    nxt = sh_ref[pl.ds(1, n_rows), :]
    kcmp = (acc_k[:, :KV_W] + nxt[:, :KV_W] + bk_ref[...]).astype(BF16)
    vcmp = (acc_v[:, :KV_W] + nxt[:, KV_W:] + bv_ref[...]).astype(BF16)

    e_c, l_c = _exp2_rows(_dot_nt(qa, jnp.concatenate([kcmp, cmpt_ref[...]], axis=1)))
    p_c = e_c * jnp.where(t_rows >= CMP_LEN - 1, 1.0 / jnp.maximum(l_c, TINY), 0.0)
    o_c = _dot(p_c.astype(BF16), vcmp)

    ps = []
    for g in range(N_KV):
        acc = p_c[(g * GROUP) * nq:(g * GROUP + 1) * nq]
        for r in range(1, GROUP):
            acc = acc + p_c[(g * GROUP + r) * nq:(g * GROUP + r + 1) * nq]
        ps.append(acc)
    ps = jnp.concatenate(ps, axis=0)
    ps_hi = ps.astype(BF16)
    ps_lo = (ps - ps_hi.astype(F32)).astype(BF16)
    imp = _dot(ps_hi, selmap_ref[...]) + _dot(ps_lo, selmap_ref[...])
    imp_t = jnp.concatenate([imp, jnp.zeros((LANES - N_KV * nq, LANES), F32)], axis=0).T
    t_lane = past + (lax.broadcasted_iota(jnp.int32, (1, LANES), 1) & (nq - 1))
    n_slc = -(-(past + nq) // SEL_LEN)
    unsel = _unselected_blocks_t(imp_t, t_lane, n_slc).T.astype(BF16)
    unsel_rows = jnp.concatenate([unsel[:nq]] * GROUP + [unsel[nq:N_KV * nq]] * GROUP, axis=0)

    zpad = jnp.zeros((page - nq, KV_W), F32)
    qs = jnp.concatenate([q2, unsel_rows], axis=1)
    s_parts = [_dot(qs, jnp.concatenate([pg[2 * KV_W:3 * KV_W, :].astype(BF16), selt_ref[p]], axis=0))
               for p, pg in enumerate(pages)]
    k_new = jnp.concatenate([kvn_ref[:, 2 * KV_W:3 * KV_W], zpad], axis=0).astype(BF16)
    v_new = jnp.concatenate([kvn_ref[:, 3 * KV_W:4 * KV_W], zpad], axis=0).astype(BF16)
    qn = jnp.concatenate([qs, onehot], axis=1)
    s_parts.append(_dot_nt(qn, jnp.concatenate([k_new, newt_ref[...]], axis=1)))
    e_s, l_s = _exp2_rows(jnp.concatenate(s_parts, axis=1))
    e_s = e_s.astype(BF16)
    acc = _dot(e_s[:, past:], v_new)
    for p, pg in enumerate(pages):
        acc = acc + _dot_nt(e_s[:, p * page:(p + 1) * page], pg[3 * KV_W:4 * KV_W, :].astype(BF16))
    o_s = acc * (1.0 / jnp.maximum(l_s, TINY))

    wb = wbuf_ref.shape[1]
    kw_new = jnp.concatenate([winn_ref[:, 0:KV_W], zpad], axis=0).astype(BF16)
    vw_new = jnp.concatenate([winn_ref[:, KV_W:], zpad], axis=0).astype(BF16)
    s_w = jnp.concatenate(
        [_dot(qa, jnp.concatenate([wbuf_ref[0:KV_W, :].astype(BF16), wint_ref[...]], axis=0)),
         _dot_nt(qa, jnp.concatenate([kw_new, newt_ref[:, LANES:]], axis=1))], axis=1)
    e_w, l_w = _exp2_rows(s_w)
    e_w = e_w.astype(BF16)
    o_w = (_dot_nt(e_w[:, :wb], wbuf_ref[KV_W:, :].astype(BF16)) + _dot(e_w[:, wb:], vw_new)) \
        * (1.0 / jnp.maximum(l_w, TINY))

    o_ref[...] = _combine_heads(gate_ref[...], o_c, o_s, o_w, nq)

    moved = pltpu.roll(wbuf_ref[...], wb - nq, 1)
    new_t = jnp.concatenate([winn_ref[...], jnp.zeros((LANES - nq, 2 * KV_W), F32)], axis=0).T
    lane = lax.broadcasted_iota(jnp.int32, (2 * KV_W, LANES), 1)
    wout_ref[:, 0:wb - LANES] = moved[:, 0:wb - LANES]
    wout_ref[:, wb - LANES:] = jnp.where(lane < LANES - nq, moved[:, wb - LANES:], pltpu.roll(new_t, LANES - nq, 1))


def _attn_sample(page_table, q, gates, kvn, winn, wbuf_t, cache_t, layer, wk, wv, bk, bv, tables, ns, win_acc):
    db, nq, _ = q.shape
    n_pages = page_table.shape[1]
    page = cache_t.shape[3]
    past = n_pages * page
    wb = wbuf_t.shape[3]
    cmpt, selt, newt, wint, onehot, selmap = tables
    seq3 = lambda s, pt: (s, 0, 0)
    const2 = lambda s, pt: (0, 0)
    const3 = lambda s, pt: (0, 0, 0)

    def page_spec(u, p):
        return pl.BlockSpec((None, None, 4 * KV_W, page), lambda s, pt: (layer, pt[s * ns + u, p], 0, 0))

    grid_spec = pltpu.PrefetchScalarGridSpec(
        num_scalar_prefetch=1,
        grid=(db // ns,),
        in_specs=[pl.BlockSpec((ns, nq, ATTN_W), seq3), pl.BlockSpec((ns, nq, GATE_PAD), seq3),
                  pl.BlockSpec((ns, nq, 4 * KV_W), seq3), pl.BlockSpec((ns, nq, 2 * KV_W), seq3),
                  pl.BlockSpec((None, ns, 2 * KV_W, wb), lambda s, pt: (layer, s, 0, 0))]
                 + [page_spec(u, p) for u in range(ns) for p in range(n_pages)]
                 + [pl.BlockSpec(wk.shape, const3), pl.BlockSpec(wv.shape, const3),
                    pl.BlockSpec((1, KV_W), const2), pl.BlockSpec((1, KV_W), const2),
                    pl.BlockSpec(cmpt.shape, const2), pl.BlockSpec(selt.shape, const3),
                    pl.BlockSpec(newt.shape, const2), pl.BlockSpec(wint.shape, const2),
                    pl.BlockSpec(onehot.shape, const2), pl.BlockSpec(selmap.shape, const2),
                    pl.BlockSpec(memory_space=pl.ANY)],
        out_specs=[pl.BlockSpec((ns, nq, ATTN_W), seq3),
                   pl.BlockSpec((None, ns, 2 * KV_W, wb), lambda s, pt: (layer, s, 0, 0))],
        scratch_shapes=[pltpu.VMEM((ns, past // CMP_STRIDE + SUBLANES, 2 * KV_W), F32),
                        pltpu.VMEM((ns, past, KV_W), F32), pltpu.VMEM((ns, past, KV_W), F32)],
    )
    return pl.pallas_call(
        functools.partial(_attn_sample_kernel, n_pages=n_pages, past=past, ns=ns),
        grid_spec=grid_spec,
        out_shape=[jax.ShapeDtypeStruct((db, nq, ATTN_W), F32),
                   jax.ShapeDtypeStruct(win_acc.shape, win_acc.dtype)],
        input_output_aliases={6 + ns * n_pages + 10: 1},
        compiler_params=_cparams("parallel"),
        name="attn_sample",
    )(page_table, q, gates, kvn, winn, wbuf_t, *([cache_t] * (ns * n_pages)), wk, wv, bk, bv,
      cmpt, selt, newt, wint, onehot, selmap, win_acc)


def _ln_silu(y, g, b):
    mu = jnp.mean(y, axis=-1, keepdims=True)
    yc = y - mu
    var = jnp.mean(yc * yc, axis=-1, keepdims=True)
    z = yc * lax.rsqrt(var + EPS) * g + b
    return z * _sigmoid(z)


CONV_HALO = 32
CONV_ROWS = 64


def _conv_prompt_kernel(u_ref, prev_ref, w_ref, b_ref, lg_ref, lb_ref, o_ref, ext_ref, *, tiles_per_seq):
    tm = u_ref.shape[0]
    first = (pl.program_id(0) % tiles_per_seq) == 0
    ext_ref[0, 0:CONV_HALO, :] = jnp.where(first, 0.0, prev_ref[...])
    ext_ref[0, CONV_HALO:CONV_HALO + tm, :] = u_ref[...]
    keep = CONV_HALO + tm - SUBLANES
    for b in range(1, SUBLANES):
        ext_ref[b, 0:keep, :] = ext_ref[0, b:b + keep, :]
    off = CONV_HALO - (CONV_WIDTH - 1)

    for r0 in range(0, tm, CONV_ROWS):
        acc = jnp.zeros((CONV_ROWS, u_ref.shape[1]), F32) + b_ref[...]
        for k in range(CONV_WIDTH):
            a, b = divmod(off + k, SUBLANES)
            lo = r0 + a * SUBLANES
            acc = acc + w_ref[k:k + 1, :] * ext_ref[b, lo:lo + CONV_ROWS, :]
        o_ref[r0:r0 + CONV_ROWS, :] = _ln_silu(acc, lg_ref[...], lb_ref[...])


def _conv_prompt(u, w, b, lg, lb, seq, tm):
    n, cc = u.shape
    const = lambda i: (0, 0)
    per_halo = tm // CONV_HALO
    return pl.pallas_call(
        functools.partial(_conv_prompt_kernel, tiles_per_seq=seq // tm),
        grid=(n // tm,),
        in_specs=[pl.BlockSpec((tm, cc), lambda i: (i, 0)),
                  pl.BlockSpec((CONV_HALO, cc), lambda i: (jnp.maximum(i * per_halo - 1, 0), 0)),
                  pl.BlockSpec(w.shape, const), pl.BlockSpec((1, cc), const),
                  pl.BlockSpec((1, cc), const), pl.BlockSpec((1, cc), const)],
        out_specs=pl.BlockSpec((tm, cc), lambda i: (i, 0)),
        out_shape=jax.ShapeDtypeStruct((n, cc), F32),
        scratch_shapes=[pltpu.VMEM((SUBLANES, CONV_HALO + tm, cc), F32)],
        compiler_params=_cparams("parallel"),
        name="conv_prompt",
    )(u, u, w, b, lg, lb)


def _conv_sample_kernel(ext_ref, w_ref, b_ref, lg_ref, lb_ref, o_ref):
    ns, nq, cc = o_ref.shape
    for s in range(ns):
        acc = jnp.zeros((nq, cc), F32) + b_ref[...]
        for k in range(CONV_WIDTH):
            acc = acc + w_ref[k:k + 1, :] * ext_ref[s, pl.ds(k, nq), :]
        o_ref[s] = _ln_silu(acc, lg_ref[...], lb_ref[...])


def _conv_sample(ext, w, b, lg, lb, nq, ns):
    db, rows, cc = ext.shape
    const = lambda i: (0, 0)
    return pl.pallas_call(
        _conv_sample_kernel,
        grid=(db // ns,),
        in_specs=[pl.BlockSpec((ns, rows, cc), lambda i: (i, 0, 0)),
                  pl.BlockSpec(w.shape, const), pl.BlockSpec((1, cc), const),
                  pl.BlockSpec((1, cc), const), pl.BlockSpec((1, cc), const)],
        out_specs=pl.BlockSpec((ns, nq, cc), lambda i: (i, 0, 0)),
        out_shape=jax.ShapeDtypeStruct((db, nq, cc), F32),
        compiler_params=_cparams("parallel"),
        name="conv_sample",
    )(ext, w, b, lg, lb)


def _merge_kernel(x_ref, gpre_ref, oa_ref, oc_ref, wm_ref, wao_ref, wco_ref, wout_ref, gpost_ref, xo_ref):
    x = x_ref[...]
    d = x.shape[1]
    zb = _rms(x, gpre_ref[...]).astype(BF16)
    gm = _sigmoid(_dot(zb, wm_ref[...]))
    a = _dot(oa_ref[...].astype(BF16), wao_ref[...])
    c = _dot(oc_ref[...].astype(BF16), wco_ref[...])
    mix = gm[:, :d] * a + gm[:, d:] * c
    h = _dot(mix.astype(BF16), wout_ref[...])
    xo_ref[...] = x + _rms(h, gpost_ref[...])


def _merge(x, gpre, oa, oc, wm, wao, wco, wout, gpost, tm):
    n, d = x.shape
    const = lambda i: (0, 0)
    row = lambda i: (i, 0)
    return pl.pallas_call(
        _merge_kernel,
        grid=(n // tm,),
        in_specs=[pl.BlockSpec((tm, d), row), pl.BlockSpec((1, d), const),
                  pl.BlockSpec((tm, oa.shape[1]), row), pl.BlockSpec((tm, oc.shape[1]), row),
                  pl.BlockSpec(wm.shape, const), pl.BlockSpec(wao.shape, const),
                  pl.BlockSpec(wco.shape, const), pl.BlockSpec(wout.shape, const),
                  pl.BlockSpec((1, d), const)],
        out_specs=pl.BlockSpec((tm, d), row),
        out_shape=jax.ShapeDtypeStruct((n, d), F32),
        compiler_params=_cparams("parallel"),
        name="merge",
    )(x, gpre, oa, oc, wm, wao, wco, wout, gpost)


FFN_HALO = SUBLANES


def _ffn_kernel(*refs, sample, tiles_per_seq, tf, nq):
    if sample:
        (x_ref, gpre_ref, wg_ref, wv_ref, cw_ref, cb_ref, wd_ref, gpost_ref, h1_ref, h2_ref,
         xo_ref, up_ref, ext_ref) = refs
    else:
        (x_ref, gpre_ref, wg_ref, wv_ref, cw_ref, cb_ref, wd_ref, gpost_ref,
         xo_ref, up_ref, ext_ref, carry_ref) = refs
    x = x_ref[...]
    tm, d = x.shape
    dff = wd_ref.shape[0]
    zb = _rms(x, gpre_ref[...]).astype(BF16)
    first = (pl.program_id(0) % tiles_per_seq) == 0
    tpos = lax.broadcasted_iota(jnp.int32, (tm, 1), 0) & (nq - 1)
    if not sample:
        @pl.when(pl.program_id(0) == 0)
        def _():
            carry_ref[...] = jnp.zeros(carry_ref.shape, F32)

    def conv_half(w_ref, c0, col0):
        up = _dot(zb, w_ref[:, c0:c0 + tf])
        ext_ref[FFN_HALO:FFN_HALO + tm, :] = up
        if sample:
            ext_ref[0:FFN_HALO, :] = jnp.zeros((FFN_HALO, tf), F32)
            up_ref[:, col0:col0 + tf] = up
        else:
            ext_ref[0:FFN_HALO, :] = jnp.where(first, 0.0, carry_ref[:, col0:col0 + tf])
            carry_ref[:, col0:col0 + tf] = up[tm - FFN_HALO:, :]
            up_ref[:, col0:col0 + tf] = up[tm - FFN_HALO:, :]
        s1 = ext_ref[pl.ds(FFN_HALO - 1, tm), :]
        s2 = ext_ref[pl.ds(FFN_HALO - 2, tm), :]
        if sample:
            s1 = jnp.where(tpos == 0, h1_ref[:, col0:col0 + tf], s1)
            s2 = jnp.where(tpos <= 1, h2_ref[:, col0:col0 + tf], s2)
        cw = cw_ref[:, col0:col0 + tf]
        return cw[0:1] * s2 + cw[1:2] * s1 + cw[2:3] * up + cb_ref[:, col0:col0 + tf]

    acc = jnp.zeros((tm, d), F32)
    for c in range(dff // tf):
        hg = conv_half(wg_ref, c * tf, c * tf)
        hv = conv_half(wv_ref, c * tf, dff + c * tf)
        act = (_gelu_tanh(hg) * hv).astype(BF16)
        acc = acc + _dot(act, wd_ref[c * tf:(c + 1) * tf, :])
    xo_ref[...] = x + _rms(acc, gpost_ref[...])


def _ffn(x, gpre, wg, wv, cw, cb, wd, gpost, tm, tf, seq=None, hist=None, nq=None):
    n, d = x.shape
    dff = wd.shape[0]
    sample = hist is not None
    const = lambda i: (0, 0)
    row = lambda i: (i, 0)
    single = pl.Buffered(1)
    in_specs = [pl.BlockSpec((tm, d), row), pl.BlockSpec((1, d), const),
                pl.BlockSpec(wg.shape, const, pipeline_mode=single),
                pl.BlockSpec(wv.shape, const, pipeline_mode=single),
                pl.BlockSpec(cw.shape, const), pl.BlockSpec(cb.shape, const),
                pl.BlockSpec(wd.shape, const, pipeline_mode=single), pl.BlockSpec((1, d), const)]
    args = [x, gpre, wg, wv, cw, cb, wd, gpost]
    scratch = [pltpu.VMEM((FFN_HALO + tm, tf), F32)]
    if sample:
        in_specs += [pl.BlockSpec((tm, 2 * dff), row)] * 2
        args += list(hist)
        up_rows, tiles_per_seq = tm, 1
    else:
        scratch.append(pltpu.VMEM((FFN_HALO, 2 * dff), F32))
        up_rows, tiles_per_seq, nq = FFN_HALO, seq // tm, tm
    return pl.pallas_call(
        functools.partial(_ffn_kernel, sample=sample, tiles_per_seq=tiles_per_seq, tf=tf, nq=nq),
        grid=(n // tm,),
        in_specs=in_specs,
        out_specs=[pl.BlockSpec((tm, d), row), pl.BlockSpec((up_rows, 2 * dff), row)],
        out_shape=[jax.ShapeDtypeStruct((n, d), F32),
                   jax.ShapeDtypeStruct((n // tm * up_rows, 2 * dff), F32)],
        scratch_shapes=scratch,
        compiler_params=_cparams("arbitrary"),
        name="ffn_sample" if sample else "ffn_prompt",
    )(*args)


def _bias(cond):
    return jnp.asarray(np.where(cond, NEG_BIG, 0.0).astype(np.float32), dtype=BF16)


def _onehot_rows(rows, nq):
    return jnp.asarray((np.arange(LANES)[None, :] == (np.arange(rows)[:, None] % nq)).astype(np.float32), dtype=BF16)


def _cmp_sel_table(n_rows, n_cmp, n_slc):
    s = np.arange(n_rows) * CMP_STRIDE
    e = s + CMP_LEN - 1
    js = np.arange(LANES) * SEL_LEN
    je = js + SEL_LEN - 1
    m = (s[:, None] <= je[None, :]) & (e[:, None] >= js[None, :])
    m &= (np.arange(n_rows)[:, None] < n_cmp) & (np.arange(LANES)[None, :] < n_slc)
    return m.astype(np.float32)


def _prompt_tables(seq, tk):
    nq = Q_BLOCK
    n_rows = seq // CMP_STRIDE
    lane = np.arange(LANES)
    n = np.arange(n_rows)
    blk = np.arange(seq // nq)
    cmpt = _bias(CMP_STRIDE * n[None, :, None] + CMP_LEN - 1 > nq * blk[:, None, None] + lane[None, None, :])
    key = np.arange(seq).reshape(seq // tk, tk)
    selt = _bias(key[:, :, None] // SEL_LEN == (lane % HEAD_DIM)[None, None, :])
    i = np.arange(tk)
    d = np.arange(tk // nq + 1)
    cast = _bias((i[None, :, None] > nq * d[:, None, None] + lane[None, None, :]) & (d < tk // nq)[:, None, None])
    span = np.arange(WINDOW + nq)
    w = np.arange(WINDOW // nq + 1)
    dt = (nq * w[:, None, None] + lane[None, None, :]) - span[None, :, None]
    wint = _bias((dt < 0) | (dt > WINDOW))
    n_cmp = (seq - CMP_LEN) // CMP_STRIDE + 1
    selmap_t = jnp.asarray(_cmp_sel_table(n_rows, n_cmp, seq // SEL_LEN).T, dtype=BF16)
    return cmpt, selt, cast, wint, _onehot_rows(N_HEADS * nq, nq), selmap_t


def _sample_tables(past, page, nq, wb):
    n_rows = past // CMP_STRIDE
    lane = np.arange(LANES)
    n = np.arange(n_rows)
    qvalid = lane < nq
    cmpt = _bias((CMP_STRIDE * n[:, None] + CMP_LEN - 1 > past + lane[None, :]) & qvalid[None, :])
    key = np.arange(past).reshape(past // page, 1, page)
    selt = _bias(key // SEL_LEN == lane[None, :, None])
    i = np.arange(page)
    new_sel = (past + i[:, None]) // SEL_LEN == lane[None, :]
    new_causal = (i[:, None] > lane[None, :]) & qvalid[None, :]
    newt = _bias(np.concatenate([new_sel, new_causal], axis=1))
    dt = (lane[:, None] + wb) - np.arange(wb)[None, :]
    wint = _bias(((dt < 0) | (dt > WINDOW)) & qvalid[:, None])
    n_cmp = (past + nq - CMP_LEN) // CMP_STRIDE + 1
    selmap = jnp.asarray(_cmp_sel_table(n_rows, n_cmp, -(-(past + nq) // SEL_LEN)), dtype=BF16)
    return cmpt, selt, newt, wint, _onehot_rows(N_HEADS * nq, nq), selmap


def _phi_weights(w):
    depth = w.shape[0]
    wj = w.reshape(depth, CMP_LEN, HEAD_DIM, HEAD_DIM)
    eye = jnp.eye(N_KV, dtype=w.dtype)
    bd = jnp.einsum("gh,ljde->ljgdhe", eye, wj).reshape(depth, CMP_LEN, KV_W, KV_W)
    return jnp.concatenate([bd[:, :CMP_STRIDE], bd[:, CMP_STRIDE:]], axis=-1).astype(BF16)


def kernel(x_prompt, x_sample, cache_kv, state_kv_win, state_conv, state_ffn, page_table, norm_mix_pre,
           norm_mix_post, norm_ffn_pre, norm_ffn_post, w_in, w_phi_k, b_phi_k, w_phi_v, b_phi_v, w_attn_out,
           conv_w, conv_b, conv_ln_g, conv_ln_b, w_conv_out, w_out, w_up, ffn_conv_w, ffn_conv_b, w_down):
    batch, seq, d = x_prompt.shape
    db, nq, _ = x_sample.shape
    depth, n_phys, page = cache_kv.shape[:3]
    n_pages = page_table.shape[1]
    past = n_pages * page
    wb = state_kv_win.shape[2]
    cc = conv_w.shape[2]
    dff = w_down.shape[1]
    assert seq % Q_BLOCK == 0 and nq == SUBLANES and page == LANES and wb == WINDOW

    tm = 512
    tm_f = 1024
    tm_s = 128
    tf = 256
    tk = 512
    ns_attn = 2
    assert dff % tf == 0 and seq % tk == 0 and tk % Q_BLOCK == 0 and WINDOW % Q_BLOCK == 0

    c1 = ATTN_W + 6 * KV_W
    c2 = c1 + 3 * N_HEADS
    c3 = c2 + 2 * cc
    wq = w_in[:, :, :c1].astype(BF16)
    wgate = jnp.pad(w_in[:, :, c1:c2], ((0, 0), (0, 0), (0, GATE_PAD - 3 * N_HEADS))).astype(BF16)
    wu = w_in[:, :, c2:c3].astype(BF16)
    wm = w_in[:, :, c3:].astype(BF16)
    wao = w_attn_out.astype(BF16)
    wco = w_conv_out.astype(BF16)
    wout = w_out.astype(BF16)
    wup_g = w_up[:, :, :dff].astype(BF16)
    wup_v = w_up[:, :, dff:].astype(BF16)
    wdn = w_down.astype(BF16)
    wk_ab = _phi_weights(w_phi_k)
    wv_ab = _phi_weights(w_phi_v)
    bk2 = jnp.tile(b_phi_k, (1, N_KV)).reshape(depth, 1, KV_W)
    bv2 = jnp.tile(b_phi_v, (1, N_KV)).reshape(depth, 1, KV_W)

    tables_p = _prompt_tables(seq, tk)
    tables_s = _sample_tables(past, page, nq, wb)

    cache_t = cache_kv.transpose(0, 1, 3, 4, 5, 2).reshape(depth, n_phys, 4 * KV_W, page)
    wbuf_t = state_kv_win.transpose(0, 1, 3, 4, 5, 2).reshape(depth, db, 2 * KV_W, wb)
    row = lambda v: v.reshape(1, -1)

    xp = x_prompt.reshape(batch * seq, d)
    xs = x_sample.reshape(db * nq, d)
    outs = [[] for _ in range(8)]
    win_acc = jnp.zeros(wbuf_t.shape, F32)
    for l in range(depth):
        gpre, gpost = row(norm_mix_pre[l]), row(norm_mix_post[l])
        fpre, fpost = row(norm_ffn_pre[l]), row(norm_ffn_post[l])
        cb, lg, lb = row(conv_b[l]), row(conv_ln_g[l]), row(conv_ln_b[l])
        fcb = row(ffn_conv_b[l])

        q, kv, win, att_b, gates, u = _in_proj(xp, gpre, wq[l], wgate[l], wu[l], tm)
        kcmp, vcmp = _compress_prompt(kv, wk_ab[l], wv_ab[l], bk2[l], bv2[l], batch, seq)
        oa = _attn_prompt(q, gates, att_b, kcmp, vcmp, tables_p, batch, seq, tk)
        oc = _conv_prompt(u, conv_w[l], cb, lg, lb, seq, tm)
        xp = _merge(xp, gpre, oa, oc, wm[l], wao[l], wco[l], wout[l], gpost, tm)
        xp, up_tail = _ffn(xp, fpre, wup_g[l], wup_v[l], ffn_conv_w[l], fcb, wdn[l], fpost, tm_f, tf, seq=seq)
        outs[0].append(kv.reshape(batch, seq, 4, N_KV, HEAD_DIM))
        outs[2].append(win.reshape(batch, seq, 2, N_KV, HEAD_DIM)[:, seq - min(WINDOW, seq):])
        outs[4].append(u.reshape(batch, seq, cc)[:, seq - (CONV_WIDTH - 1):])
        outs[6].append(up_tail.reshape(batch, seq // tm_f, FFN_HALO, 2 * dff)[:, -1, FFN_HALO - (FFN_CONV_WIDTH - 1):])

        q, kv, win, _, gates, u = _in_proj(xs, gpre, wq[l], wgate[l], wu[l], tm)
        oa, win_acc = _attn_sample(page_table, q.reshape(db, nq, ATTN_W), gates.reshape(db, nq, GATE_PAD),
                          kv.reshape(db, nq, 4 * KV_W), win.reshape(db, nq, 2 * KV_W), wbuf_t, cache_t, l,
                          wk_ab[l], wv_ab[l], bk2[l], bv2[l], tables_s, ns_attn, win_acc)
        ext = jnp.concatenate([state_conv[l], u.reshape(db, nq, cc)], axis=1)
        oc = _conv_sample(ext, conv_w[l], cb, lg, lb, nq, 16)
        xs = _merge(xs, gpre, oa.reshape(db * nq, ATTN_W), oc.reshape(db * nq, cc),
                    wm[l], wao[l], wco[l], wout[l], gpost, tm)
        hist = state_ffn[l]
        h2 = jnp.pad(hist, ((0, 0), (0, nq - (FFN_CONV_WIDTH - 1)), (0, 0))).reshape(db * nq, 2 * dff)
        h1 = jnp.pad(hist[:, 1:], ((0, 0), (0, nq - 1), (0, 0))).reshape(db * nq, 2 * dff)
        xs, up_s = _ffn(xs, fpre, wup_g[l], wup_v[l], ffn_conv_w[l], fcb, wdn[l], fpost, tm_s, tf,
                        hist=(h1, h2), nq=nq)
        outs[1].append(kv.reshape(db, nq, 4, N_KV, HEAD_DIM))
        outs[5].append(ext[:, nq:])
        outs[7].append(up_s.reshape(db, nq, 2 * dff)[:, nq - (FFN_CONV_WIDTH - 1):])

    return (xp.reshape(batch, seq, d), xs.reshape(db, nq, d),
            jnp.stack(outs[0]), jnp.stack(outs[1]), jnp.stack(outs[2]),
            win_acc.reshape(depth, db, 2, N_KV, HEAD_DIM, wb).transpose(0, 1, 5, 2, 3, 4),
            jnp.stack(outs[4]), jnp.stack(outs[5]), jnp.stack(outs[6]), jnp.stack(outs[7]))
```

```python
import functools

import numpy as np
import jax
import jax.numpy as jnp
from jax import lax
from jax.experimental import pallas as pl
from jax.experimental.pallas import tpu as pltpu

F32 = jnp.float32
BF16 = jnp.bfloat16

N_HEADS = 8
N_KV = 2
GROUP = N_HEADS // N_KV
HEAD_DIM = 64
ATTN_W = N_HEADS * HEAD_DIM
KV_W = N_KV * HEAD_DIM
CMP_LEN = 32
CMP_STRIDE = 16
SEL_LEN = 64
N_SEL = 8
WINDOW = 512
Q_BLOCK = 128
FORCE_BONUS = 1000.0
CONV_WIDTH = 31
FFN_CONV_WIDTH = 3
EPS = 1e-6
TINY = 1e-30
NEG_BIG = -(2.0 ** 100)
LOG2E = 1.4426950408889634
GATE_PAD = 128

LANES = 128
SUBLANES = 8
VMEM_LIMIT = 56 * 1024 * 1024


def _cparams(*sem):
    return pltpu.CompilerParams(dimension_semantics=sem, vmem_limit_bytes=VMEM_LIMIT)


def _rms(x, g):
    return x * lax.rsqrt(jnp.mean(x * x, axis=-1, keepdims=True) + EPS) * g


def _sigmoid(x):
    return 1.0 / (1.0 + jnp.exp(-x))


def _gelu_tanh(x):
    c = np.sqrt(2.0 / np.pi).astype(np.float32)
    return x * (0.5 * (1.0 + jnp.tanh(c * (x + 0.044715 * (x * x * x)))))


def _dot(a, b):
    return jnp.dot(a, b, preferred_element_type=F32)


def _dot_nt(a, b):
    return lax.dot_general(a, b, (((1,), (1,)), ((), ())), preferred_element_type=F32)


def _exp2_rows(s):
    m = jnp.max(s, axis=-1, keepdims=True)
    e = jnp.exp2(s - m)
    return e, jnp.sum(e, axis=-1, keepdims=True)


def _build_q2(q, nq):
    lane = lax.broadcasted_iota(jnp.int32, (nq, LANES), 1)
    parts = []
    for h in range(N_HEADS):
        g = h // GROUP
        c = h // 2
        chunk = q[:, c * LANES:(c + 1) * LANES]
        if (h % 2) != g:
            chunk = pltpu.roll(chunk, HEAD_DIM, 1)
        keep = (lane >= g * HEAD_DIM) & (lane < (g + 1) * HEAD_DIM)
        parts.append(jnp.where(keep, chunk * (HEAD_DIM ** -0.5 * LOG2E), 0.0))
    return jnp.concatenate(parts, axis=0).astype(BF16)


def _row_positions(nq, t0):
    r = lax.broadcasted_iota(jnp.int32, (N_HEADS * nq, 1), 0)
    return t0 + (r & (nq - 1))


def _unselected_blocks_t(imp_t, t_lane, n_slc):
    jb = lax.broadcasted_iota(jnp.int32, imp_t.shape, 0)
    cur = t_lane >> 6
    valid = (jb * SEL_LEN) <= t_lane
    forced = (jb == 0) | (jb == cur) | (jb == cur - 1)
    score = jnp.where(valid, imp_t + FORCE_BONUS * forced.astype(F32), -1.0)
    score = jnp.where(jb < n_slc, score, -2.0)
    jbf = jb.astype(F32)
    unsel = jnp.ones(imp_t.shape, F32)
    for _ in range(N_SEL):
        m = jnp.max(score, axis=0, keepdims=True)
        idx = jnp.min(jnp.where(score == m, jbf, 1e9), axis=0, keepdims=True)
        hit = jbf == idx
        unsel = jnp.where(hit, 0.0, unsel)
        score = jnp.where(hit, -3.0, score)
    return unsel


def _split_dot(w, x):
    hi = x.astype(BF16)
    lo = (x - hi.astype(F32)).astype(BF16)
    return _dot(w, hi) + _dot(w, lo)


def _combine_heads(gates, o_c, o_s, o_w, nq):
    lane = lax.broadcasted_iota(jnp.int32, (nq, LANES), 1)
    heads = []
    for h in range(N_HEADS):
        sl = slice(h * nq, (h + 1) * nq)
        o = (gates[:, h:h + 1] * o_c[sl]
             + gates[:, N_HEADS + h:N_HEADS + h + 1] * o_s[sl]
             + gates[:, 2 * N_HEADS + h:2 * N_HEADS + h + 1] * o_w[sl])
        if (h % 2) != (h // GROUP):
            o = pltpu.roll(o, HEAD_DIM, 1)
        heads.append(o)
    chunks = [jnp.where(lane < HEAD_DIM, heads[2 * c], heads[2 * c + 1]) for c in range(N_HEADS // 2)]
    return jnp.concatenate(chunks, axis=1)


def _in_proj_kernel(x_ref, g_ref, wq_ref, wg_ref, wu_ref, q_ref, kv_ref, win_ref, att_ref, gate_ref, u_ref):
    zb = _rms(x_ref[...], g_ref[...]).astype(BF16)
    qkv = _dot(zb, wq_ref[...])
    q_ref[...] = qkv[:, :ATTN_W]
    kv_ref[...] = qkv[:, ATTN_W:ATTN_W + 4 * KV_W]
    win_ref[...] = qkv[:, ATTN_W + 4 * KV_W:]
    att_ref[...] = qkv[:, ATTN_W + 2 * KV_W:].astype(BF16)
    gate_ref[...] = _sigmoid(_dot(zb, wg_ref[...]))
    u2 = _dot(zb, wu_ref[...])
    cc = u2.shape[1] // 2
    u_ref[...] = u2[:, :cc] * _sigmoid(u2[:, cc:])


def _in_proj(x, g, wq, wg, wu, tm):
    n, d = x.shape
    cc = wu.shape[1] // 2
    const = lambda i: (0, 0)
    row = lambda i: (i, 0)
    return pl.pallas_call(
        _in_proj_kernel,
        grid=(n // tm,),
        in_specs=[pl.BlockSpec((tm, d), row), pl.BlockSpec((1, d), const),
                  pl.BlockSpec(wq.shape, const), pl.BlockSpec(wg.shape, const), pl.BlockSpec(wu.shape, const)],
        out_specs=[pl.BlockSpec((tm, ATTN_W), row), pl.BlockSpec((tm, 4 * KV_W), row),
                   pl.BlockSpec((tm, 2 * KV_W), row), pl.BlockSpec((tm, 4 * KV_W), row),
                   pl.BlockSpec((tm, GATE_PAD), row),
                   pl.BlockSpec((tm, cc), row)],
        out_shape=[jax.ShapeDtypeStruct((n, ATTN_W), F32), jax.ShapeDtypeStruct((n, 4 * KV_W), F32),
                   jax.ShapeDtypeStruct((n, 2 * KV_W), F32), jax.ShapeDtypeStruct((n, 4 * KV_W), BF16),
                   jax.ShapeDtypeStruct((n, GATE_PAD), F32),
                   jax.ShapeDtypeStruct((n, cc), F32)],
        compiler_params=_cparams("parallel"),
        name="in_proj",
    )(x, g, wq, wg, wu)


def _compress_prompt_kernel(k_ref, v_ref, wk_ref, wv_ref, bk_ref, bv_ref, kc_ref, vc_ref, sh_ref, *, n_rows):
    acc_k = jnp.zeros((n_rows, 2 * KV_W), F32)
    acc_v = jnp.zeros((n_rows, 2 * KV_W), F32)
    for j in range(CMP_STRIDE):
        rk = k_ref[pl.ds(j, n_rows, stride=CMP_STRIDE), :].astype(BF16)
        rv = v_ref[pl.ds(j, n_rows, stride=CMP_STRIDE), :].astype(BF16)
        acc_k = acc_k + _dot(rk, wk_ref[j])
        acc_v = acc_v + _dot(rv, wv_ref[j])
    sh_ref[0:n_rows, 0:KV_W] = acc_k[:, KV_W:]
    sh_ref[0:n_rows, KV_W:] = acc_v[:, KV_W:]
    sh_ref[n_rows:n_rows + SUBLANES, :] = jnp.zeros((SUBLANES, 2 * KV_W), F32)
    nxt = sh_ref[pl.ds(1, n_rows), :]
    kc_ref[...] = acc_k[:, :KV_W] + nxt[:, :KV_W] + bk_ref[...]
    vc_ref[...] = acc_v[:, :KV_W] + nxt[:, KV_W:] + bv_ref[...]


def _compress_prompt(kv, wk, wv, bk, bv, batch, seq):
    n_rows = seq // CMP_STRIDE
    const2 = lambda b: (0, 0)
    const3 = lambda b: (0, 0, 0)
    return pl.pallas_call(
        functools.partial(_compress_prompt_kernel, n_rows=n_rows),
        grid=(batch,),
        in_specs=[pl.BlockSpec((seq, KV_W), lambda b: (b, 0)), pl.BlockSpec((seq, KV_W), lambda b: (b, 1)),
                  pl.BlockSpec(wk.shape, const3), pl.BlockSpec(wv.shape, const3),
                  pl.BlockSpec((1, KV_W), const2), pl.BlockSpec((1, KV_W), const2)],
        out_specs=[pl.BlockSpec((n_rows, KV_W), lambda b: (b, 0))] * 2,
        out_shape=[jax.ShapeDtypeStruct((batch * n_rows, KV_W), F32)] * 2,
        scratch_shapes=[pltpu.VMEM((n_rows + SUBLANES, 2 * KV_W), F32)],
        compiler_params=_cparams("parallel"),
        name="compress_prompt",
    )(kv, kv, wk, wv, bk, bv)


def _softmax_pv(s, v_aug):
    m = jnp.max(s, axis=-1, keepdims=True)
    return _dot(jnp.exp2(s - m), v_aug)


def _group_values(v):
    lane = lax.broadcasted_iota(jnp.int32, v.shape, 1)
    one = jnp.ones(v.shape, v.dtype)
    return jnp.where(lane < HEAD_DIM, v, one), jnp.where(lane >= HEAD_DIM, v, one)


def _normalise(acc):
    return acc * (1.0 / jnp.maximum(pltpu.roll(acc, HEAD_DIM, 1), TINY))


def _attn_prompt_kernel(q_ref, gate_ref, ksv_ref, kwv_ref, kc_ref, vc_ref, cmpt_ref, selt_ref, cast_ref,
                        wint_ref, onehot_ref, selmap_ref, o_ref, *, seq, tk):
    nq = Q_BLOCK
    rows = N_HEADS * nq
    half = rows // N_KV
    iblk = pl.program_id(1)
    start = iblk * nq
    q2 = _build_q2(q_ref[...], nq)
    onehot = onehot_ref[...]
    qa = jnp.concatenate([q2, onehot], axis=1)
    t_rows = _row_positions(nq, start)

    def two_groups(s, v):
        v0, v1 = _group_values(v)
        return jnp.concatenate([_softmax_pv(s[:half], v0), _softmax_pv(s[half:], v1)], axis=0)

    kca = jnp.concatenate([kc_ref[...].astype(BF16), cmpt_ref[...]], axis=1)
    s_c = _dot_nt(qa, kca).astype(BF16)
    o_c = jnp.where(t_rows >= CMP_LEN - 1, _normalise(two_groups(s_c, vc_ref[...].astype(BF16))), 0.0)

    s_t = _dot_nt(kca, qa)
    e_t = jnp.exp2(s_t - jnp.max(s_t, axis=0, keepdims=True))
    l_t = jnp.sum(e_t, axis=0, keepdims=True)
    t_lane = start + (lax.broadcasted_iota(jnp.int32, (1, rows), 1) & (nq - 1))
    p_t = e_t * jnp.where(t_lane >= CMP_LEN - 1, 1.0 / jnp.maximum(l_t, TINY), 0.0)
    ps_t = []
    for g in range(N_KV):
        acc = p_t[:, (g * GROUP) * nq:(g * GROUP + 1) * nq]
        for r in range(1, GROUP):
            acc = acc + p_t[:, (g * GROUP + r) * nq:(g * GROUP + r + 1) * nq]
        ps_t.append(acc)
    ps_t = jnp.concatenate(ps_t, axis=1)
    imp_t = _split_dot(selmap_ref[...], ps_t)
    unsel = _unselected_blocks_t(imp_t, t_lane[:, :N_KV * nq], seq // SEL_LEN).T

    lane_q = lax.broadcasted_iota(jnp.int32, (half, LANES), 1)
    lane_k = lax.broadcasted_iota(jnp.int32, (tk, LANES), 1)
    bits0 = pltpu.roll(unsel[:nq], HEAD_DIM, 1).astype(BF16)
    bits1 = unsel[nq:].astype(BF16)
    lhs = (jnp.where(lane_q < HEAD_DIM, q2[:half], jnp.concatenate([bits0] * GROUP, axis=0)),
           jnp.where(lane_q >= HEAD_DIM, q2[half:], jnp.concatenate([bits1] * GROUP, axis=0)))

    lhs = tuple(jnp.concatenate([x, onehot[:half]], axis=1) for x in lhs)
    n_full = start // tk

    def scores(j):
        k0 = pl.multiple_of(j * tk, tk)
        ks = ksv_ref[pl.ds(k0, tk), 0:KV_W]
        tab = selt_ref[j]
        causal = cast_ref[jnp.where(j == n_full, iblk - n_full * (tk // nq), tk // nq)]
        rhs = (jnp.concatenate([jnp.where(lane_k < HEAD_DIM, ks, tab), causal], axis=1),
               jnp.concatenate([jnp.where(lane_k >= HEAD_DIM, ks, tab), causal], axis=1))
        return tuple(_dot_nt(lhs[g], rhs[g]).astype(BF16) for g in range(N_KV))

    def consume(j, s, carry):
        k0 = pl.multiple_of(j * tk, tk)
        vals = _group_values(ksv_ref[pl.ds(k0, tk), KV_W:2 * KV_W])
        out = []
        for g in range(N_KV):
            m, acc = carry[g]
            m_new = jnp.maximum(m, jnp.max(s[g], axis=-1, keepdims=True).astype(F32))
            p = jnp.exp2(s[g] - m_new.astype(BF16))
            out.append((m_new, jnp.exp2(m - m_new) * acc + _dot(p, vals[g])))
        return tuple(out)

    def body(j, state):
        s, carry = state
        return scores(j + 1), consume(j, s, carry)

    init = tuple((jnp.full((half, 1), -3.0e38, F32), jnp.zeros((half, KV_W), F32)) for _ in range(N_KV))
    s_last, carry = lax.fori_loop(0, n_full, body, (scores(0), init))
    carry = consume(n_full, s_last, carry)
    o_s = _normalise(jnp.concatenate([carry[g][1] for g in range(N_KV)], axis=0))

    span = WINDOW + nq
    ws = pl.multiple_of(jnp.maximum(start - WINDOW, 0), nq)
    kt = jnp.concatenate([kwv_ref[pl.ds(ws, span), 0:KV_W], wint_ref[jnp.minimum(iblk, WINDOW // nq)]], axis=1)
    s_w = _dot_nt(qa, kt).astype(BF16)
    o_w = _normalise(two_groups(s_w, kwv_ref[pl.ds(ws, span), KV_W:2 * KV_W]))

    o_ref[...] = _combine_heads(gate_ref[...], o_c, o_s, o_w, nq)


def _attn_prompt(q, gates, att_b, kcmp, vcmp, tables, batch, seq, tk):
    nb = seq // Q_BLOCK
    n_cmp = kcmp.shape[0] // batch
    cmpt, selt, cast, wint, onehot, selmap_t = tables
    blk = lambda b, i: (b * nb + i, 0)
    per_b = lambda b, i: (b, 0)
    const = lambda b, i: (0, 0)
    const3 = lambda b, i: (0, 0, 0)
    return pl.pallas_call(
        functools.partial(_attn_prompt_kernel, seq=seq, tk=tk),
        grid=(batch, nb),
        in_specs=[pl.BlockSpec((Q_BLOCK, ATTN_W), blk), pl.BlockSpec((Q_BLOCK, GATE_PAD), blk),
                  pl.BlockSpec((seq, 2 * KV_W), lambda b, i: (b, 0)),
                  pl.BlockSpec((seq, 2 * KV_W), lambda b, i: (b, 1)),
                  pl.BlockSpec((n_cmp, KV_W), per_b), pl.BlockSpec((n_cmp, KV_W), per_b),
                  pl.BlockSpec((None, n_cmp, LANES), lambda b, i: (i, 0, 0)),
                  pl.BlockSpec(selt.shape, const3), pl.BlockSpec(cast.shape, const3),
                  pl.BlockSpec(wint.shape, const3), pl.BlockSpec(onehot.shape, const),
                  pl.BlockSpec(selmap_t.shape, const)],
        out_specs=pl.BlockSpec((Q_BLOCK, ATTN_W), blk),
        out_shape=jax.ShapeDtypeStruct((batch * seq, ATTN_W), F32),
        compiler_params=_cparams("parallel", "parallel"),
        name="attn_prompt",
    )(q, gates, att_b, att_b, kcmp, vcmp, cmpt, selt, cast, wint, onehot, selmap_t)


def _attn_sample_kernel(pt_ref, q_ref, gate_ref, kvn_ref, winn_ref, wbuf_ref, *rest, n_pages, past, ns):
    del pt_ref
    consts = rest[ns * n_pages:ns * n_pages + 10]
    o_ref, wout_ref, sh_ref, kraw_ref, vraw_ref = rest[ns * n_pages + 11:]
    for u in range(ns):
        _attn_sample_one(q_ref.at[u], gate_ref.at[u], kvn_ref.at[u], winn_ref.at[u], wbuf_ref.at[u],
                         rest[u * n_pages:(u + 1) * n_pages], consts, o_ref.at[u], wout_ref.at[u], sh_ref.at[u],
                         kraw_ref.at[u], vraw_ref.at[u], past)


def _attn_sample_one(q_ref, gate_ref, kvn_ref, winn_ref, wbuf_ref, pages, consts, o_ref, wout_ref, sh_ref,
                     kraw_ref, vraw_ref, past):
    wk_ref, wv_ref, bk_ref, bv_ref, cmpt_ref, selt_ref, newt_ref, wint_ref, onehot_ref, selmap_ref = consts
    nq = q_ref.shape[0]
    page = pages[0].shape[1]
    n_rows = past // CMP_STRIDE
    q2 = _build_q2(q_ref[...], nq)
    onehot = onehot_ref[...]
    qa = jnp.concatenate([q2, onehot], axis=1)
    t_rows = _row_positions(nq, past)

    for p, pg in enumerate(pages):
        kraw_ref[p * page:(p + 1) * page, :] = pg[0:KV_W, :].T
        vraw_ref[p * page:(p + 1) * page, :] = pg[KV_W:2 * KV_W, :].T
    acc_k = jnp.zeros((n_rows, 2 * KV_W), F32)
    acc_v = jnp.zeros((n_rows, 2 * KV_W), F32)
    for j in range(CMP_STRIDE):
        rk = kraw_ref[pl.ds(j, n_rows, stride=CMP_STRIDE), :]
        rv = vraw_ref[pl.ds(j, n_rows, stride=CMP_STRIDE), :]
        acc_k = acc_k + _dot(rk.astype(BF16), wk_ref[j])
        acc_v = acc_v + _dot(rv.astype(BF16), wv_ref[j])
    sh_ref[0:n_rows, 0:KV_W] = acc_k[:, KV_W:]
    sh_ref[0:n_rows, KV_W:] = acc_v[:, KV_W:]
    sh_ref[n_rows:n_rows + SUBLANES, :] = jnp.zeros((SUBLANES, 2 * KV_W), F32)
    nxt = sh_ref[pl.ds(1, n_rows), :]
    kcmp = (acc_k[:, :KV_W] + nxt[:, :KV_W] + bk_ref[...]).astype(BF16)
    vcmp = (acc_v[:, :KV_W] + nxt[:, KV_W:] + bv_ref[...]).astype(BF16)

    e_c, l_c = _exp2_rows(_dot_nt(qa, jnp.concatenate([kcmp, cmpt_ref[...]], axis=1)))
    p_c = e_c * jnp.where(t_rows >= CMP_LEN - 1, 1.0 / jnp.maximum(l_c, TINY), 0.0)
    o_c = _dot(p_c.astype(BF16), vcmp)

    ps = []
    for g in range(N_KV):
        acc = p_c[(g * GROUP) * nq:(g * GROUP + 1) * nq]
        for r in range(1, GROUP):
            acc = acc + p_c[(g * GROUP + r) * nq:(g * GROUP + r + 1) * nq]
        ps.append(acc)
    ps = jnp.concatenate(ps, axis=0)
    ps_hi = ps.astype(BF16)
    ps_lo = (ps - ps_hi.astype(F32)).astype(BF16)
    imp = _dot(ps_hi, selmap_ref[...]) + _dot(ps_lo, selmap_ref[...])
    imp_t = jnp.concatenate([imp, jnp.zeros((LANES - N_KV * nq, LANES), F32)], axis=0).T
    t_lane = past + (lax.broadcasted_iota(jnp.int32, (1, LANES), 1) & (nq - 1))
    n_slc = -(-(past + nq) // SEL_LEN)
    unsel = _unselected_blocks_t(imp_t, t_lane, n_slc).T.astype(BF16)
    unsel_rows = jnp.concatenate([unsel[:nq]] * GROUP + [unsel[nq:N_KV * nq]] * GROUP, axis=0)

    zpad = jnp.zeros((page - nq, KV_W), F32)
    qs = jnp.concatenate([q2, unsel_rows], axis=1)
    s_parts = [_dot(qs, jnp.concatenate([pg[2 * KV_W:3 * KV_W, :].astype(BF16), selt_ref[p]], axis=0))
               for p, pg in enumerate(pages)]
    k_new = jnp.concatenate([kvn_ref[:, 2 * KV_W:3 * KV_W], zpad], axis=0).astype(BF16)
    v_new = jnp.concatenate([kvn_ref[:, 3 * KV_W:4 * KV_W], zpad], axis=0).astype(BF16)
    qn = jnp.concatenate([qs, onehot], axis=1)
    s_parts.append(_dot_nt(qn, jnp.concatenate([k_new, newt_ref[...]], axis=1)))
    e_s, l_s = _exp2_rows(jnp.concatenate(s_parts, axis=1))
    e_s = e_s.astype(BF16)
    acc = _dot(e_s[:, past:], v_new)
    for p, pg in enumerate(pages):
        acc = acc + _dot_nt(e_s[:, p * page:(p + 1) * page], pg[3 * KV_W:4 * KV_W, :].astype(BF16))
    o_s = acc * (1.0 / jnp.maximum(l_s, TINY))

    wb = wbuf_ref.shape[1]
    kw_new = jnp.concatenate([winn_ref[:, 0:KV_W], zpad], axis=0).astype(BF16)
    vw_new = jnp.concatenate([winn_ref[:, KV_W:], zpad], axis=0).astype(BF16)
    s_w = jnp.concatenate(
        [_dot(qa, jnp.concatenate([wbuf_ref[0:KV_W, :].astype(BF16), wint_ref[...]], axis=0)),
         _dot_nt(qa, jnp.concatenate([kw_new, newt_ref[:, LANES:]], axis=1))], axis=1)
    e_w, l_w = _exp2_rows(s_w)
    e_w = e_w.astype(BF16)
    o_w = (_dot_nt(e_w[:, :wb], wbuf_ref[KV_W:, :].astype(BF16)) + _dot(e_w[:, wb:], vw_new)) \
        * (1.0 / jnp.maximum(l_w, TINY))

    o_ref[...] = _combine_heads(gate_ref[...], o_c, o_s, o_w, nq)

    moved = pltpu.roll(wbuf_ref[...], wb - nq, 1)
    new_t = jnp.concatenate([winn_ref[...], jnp.zeros((LANES - nq, 2 * KV_W), F32)], axis=0).T
    lane = lax.broadcasted_iota(jnp.int32, (2 * KV_W, LANES), 1)
    wout_ref[:, 0:wb - LANES] = moved[:, 0:wb - LANES]
    wout_ref[:, wb - LANES:] = jnp.where(lane < LANES - nq, moved[:, wb - LANES:], pltpu.roll(new_t, LANES - nq, 1))


def _attn_sample(page_table, q, gates, kvn, winn, wbuf_t, cache_t, layer, wk, wv, bk, bv, tables, ns, win_acc):
    db, nq, _ = q.shape
    n_pages = page_table.shape[1]
    page = cache_t.shape[3]
    past = n_pages * page
    wb = wbuf_t.shape[3]
    cmpt, selt, newt, wint, onehot, selmap = tables
    seq3 = lambda s, pt: (s, 0, 0)
    const2 = lambda s, pt: (0, 0)
    const3 = lambda s, pt: (0, 0, 0)

    def page_spec(u, p):
        return pl.BlockSpec((None, None, 4 * KV_W, page), lambda s, pt: (layer, pt[s * ns + u, p], 0, 0))

    grid_spec = pltpu.PrefetchScalarGridSpec(
        num_scalar_prefetch=1,
        grid=(db // ns,),
        in_specs=[pl.BlockSpec((ns, nq, ATTN_W), seq3), pl.BlockSpec((ns, nq, GATE_PAD), seq3),
                  pl.BlockSpec((ns, nq, 4 * KV_W), seq3), pl.BlockSpec((ns, nq, 2 * KV_W), seq3),
                  pl.BlockSpec((None, ns, 2 * KV_W, wb), lambda s, pt: (layer, s, 0, 0))]
                 + [page_spec(u, p) for u in range(ns) for p in range(n_pages)]
                 + [pl.BlockSpec(wk.shape, const3), pl.BlockSpec(wv.shape, const3),
                    pl.BlockSpec((1, KV_W), const2), pl.BlockSpec((1, KV_W), const2),
                    pl.BlockSpec(cmpt.shape, const2), pl.BlockSpec(selt.shape, const3),
                    pl.BlockSpec(newt.shape, const2), pl.BlockSpec(wint.shape, const2),
                    pl.BlockSpec(onehot.shape, const2), pl.BlockSpec(selmap.shape, const2),
                    pl.BlockSpec(memory_space=pl.ANY)],
        out_specs=[pl.BlockSpec((ns, nq, ATTN_W), seq3),
                   pl.BlockSpec((None, ns, 2 * KV_W, wb), lambda s, pt: (layer, s, 0, 0))],
        scratch_shapes=[pltpu.VMEM((ns, past // CMP_STRIDE + SUBLANES, 2 * KV_W), F32),
                        pltpu.VMEM((ns, past, KV_W), F32), pltpu.VMEM((ns, past, KV_W), F32)],
    )
    return pl.pallas_call(
        functools.partial(_attn_sample_kernel, n_pages=n_pages, past=past, ns=ns),
        grid_spec=grid_spec,
        out_shape=[jax.ShapeDtypeStruct((db, nq, ATTN_W), F32),
                   jax.ShapeDtypeStruct(win_acc.shape, win_acc.dtype)],
        input_output_aliases={6 + ns * n_pages + 10: 1},
        compiler_params=_cparams("parallel"),
        name="attn_sample",
    )(page_table, q, gates, kvn, winn, wbuf_t, *([cache_t] * (ns * n_pages)), wk, wv, bk, bv,
      cmpt, selt, newt, wint, onehot, selmap, win_acc)


def _ln_silu(y, g, b):
    mu = jnp.mean(y, axis=-1, keepdims=True)
    yc = y - mu
    var = jnp.mean(yc * yc, axis=-1, keepdims=True)
    z = yc * lax.rsqrt(var + EPS) * g + b
    return z * _sigmoid(z)


CONV_HALO = 32
CONV_ROWS = 64


def _conv_prompt_kernel(u_ref, prev_ref, w_ref, b_ref, lg_ref, lb_ref, o_ref, ext_ref, *, tiles_per_seq):
    tm = u_ref.shape[0]
    first = (pl.program_id(0) % tiles_per_seq) == 0
    ext_ref[0, 0:CONV_HALO, :] = jnp.where(first, 0.0, prev_ref[...])
    ext_ref[0, CONV_HALO:CONV_HALO + tm, :] = u_ref[...]
    keep = CONV_HALO + tm - SUBLANES
    for b in range(1, SUBLANES):
        ext_ref[b, 0:keep, :] = ext_ref[0, b:b + keep, :]
    off = CONV_HALO - (CONV_WIDTH - 1)

    for r0 in range(0, tm, CONV_ROWS):
        acc = jnp.zeros((CONV_ROWS, u_ref.shape[1]), F32) + b_ref[...]
        for k in range(CONV_WIDTH):
            a, b = divmod(off + k, SUBLANES)
            lo = r0 + a * SUBLANES
            acc = acc + w_ref[k:k + 1, :] * ext_ref[b, lo:lo + CONV_ROWS, :]
        o_ref[r0:r0 + CONV_ROWS, :] = _ln_silu(acc, lg_ref[...], lb_ref[...])


def _conv_prompt(u, w, b, lg, lb, seq, tm):
    n, cc = u.shape
    const = lambda i: (0, 0)
    per_halo = tm // CONV_HALO
    return pl.pallas_call(
        functools.partial(_conv_prompt_kernel, tiles_per_seq=seq // tm),
        grid=(n // tm,),
        in_specs=[pl.BlockSpec((tm, cc), lambda i: (i, 0)),
                  pl.BlockSpec((CONV_HALO, cc), lambda i: (jnp.maximum(i * per_halo - 1, 0), 0)),
                  pl.BlockSpec(w.shape, const), pl.BlockSpec((1, cc), const),
                  pl.BlockSpec((1, cc), const), pl.BlockSpec((1, cc), const)],
        out_specs=pl.BlockSpec((tm, cc), lambda i: (i, 0)),
        out_shape=jax.ShapeDtypeStruct((n, cc), F32),
        scratch_shapes=[pltpu.VMEM((SUBLANES, CONV_HALO + tm, cc), F32)],
        compiler_params=_cparams("parallel"),
        name="conv_prompt",
    )(u, u, w, b, lg, lb)


def _conv_sample_kernel(ext_ref, w_ref, b_ref, lg_ref, lb_ref, o_ref):
    ns, nq, cc = o_ref.shape
    for s in range(ns):
        acc = jnp.zeros((nq, cc), F32) + b_ref[...]
        for k in range(CONV_WIDTH):
            acc = acc + w_ref[k:k + 1, :] * ext_ref[s, pl.ds(k, nq), :]
        o_ref[s] = _ln_silu(acc, lg_ref[...], lb_ref[...])


def _conv_sample(ext, w, b, lg, lb, nq, ns):
    db, rows, cc = ext.shape
    const = lambda i: (0, 0)
    return pl.pallas_call(
        _conv_sample_kernel,
        grid=(db // ns,),
        in_specs=[pl.BlockSpec((ns, rows, cc), lambda i: (i, 0, 0)),
                  pl.BlockSpec(w.shape, const), pl.BlockSpec((1, cc), const),
                  pl.BlockSpec((1, cc), const), pl.BlockSpec((1, cc), const)],
        out_specs=pl.BlockSpec((ns, nq, cc), lambda i: (i, 0, 0)),
        out_shape=jax.ShapeDtypeStruct((db, nq, cc), F32),
        compiler_params=_cparams("parallel"),
        name="conv_sample",
    )(ext, w, b, lg, lb)


def _merge_kernel(x_ref, gpre_ref, oa_ref, oc_ref, wm_ref, wao_ref, wco_ref, wout_ref, gpost_ref, xo_ref):
    x = x_ref[...]
    d = x.shape[1]
    zb = _rms(x, gpre_ref[...]).astype(BF16)
    gm = _sigmoid(_dot(zb, wm_ref[...]))
    a = _dot(oa_ref[...].astype(BF16), wao_ref[...])
    c = _dot(oc_ref[...].astype(BF16), wco_ref[...])
    mix = gm[:, :d] * a + gm[:, d:] * c
    h = _dot(mix.astype(BF16), wout_ref[...])
    xo_ref[...] = x + _rms(h, gpost_ref[...])


def _merge(x, gpre, oa, oc, wm, wao, wco, wout, gpost, tm):
    n, d = x.shape
    const = lambda i: (0, 0)
    row = lambda i: (i, 0)
    return pl.pallas_call(
        _merge_kernel,
        grid=(n // tm,),
        in_specs=[pl.BlockSpec((tm, d), row), pl.BlockSpec((1, d), const),
                  pl.BlockSpec((tm, oa.shape[1]), row), pl.BlockSpec((tm, oc.shape[1]), row),
                  pl.BlockSpec(wm.shape, const), pl.BlockSpec(wao.shape, const),
                  pl.BlockSpec(wco.shape, const), pl.BlockSpec(wout.shape, const),
                  pl.BlockSpec((1, d), const)],
        out_specs=pl.BlockSpec((tm, d), row),
        out_shape=jax.ShapeDtypeStruct((n, d), F32),
        compiler_params=_cparams("parallel"),
        name="merge",
    )(x, gpre, oa, oc, wm, wao, wco, wout, gpost)


FFN_HALO = SUBLANES


def _ffn_kernel(*refs, sample, tiles_per_seq, tf, nq):
    if sample:
        (x_ref, gpre_ref, wg_ref, wv_ref, cw_ref, cb_ref, wd_ref, gpost_ref, h1_ref, h2_ref,
         xo_ref, up_ref, ext_ref, act_ref) = refs
    else:
        (x_ref, gpre_ref, wg_ref, wv_ref, cw_ref, cb_ref, wd_ref, gpost_ref,
         xo_ref, up_ref, ext_ref, act_ref, carry_ref) = refs
    x = x_ref[...]
    tm, d = x.shape
    dff = wd_ref.shape[0]
    zb = _rms(x, gpre_ref[...]).astype(BF16)
    first = (pl.program_id(0) % tiles_per_seq) == 0
    tpos = lax.broadcasted_iota(jnp.int32, (tm, 1), 0) & (nq - 1)
    if not sample:
        @pl.when(pl.program_id(0) == 0)
        def _():
            carry_ref[...] = jnp.zeros(carry_ref.shape, F32)

    def conv_half(w_ref, c0, col0):
        up = _dot(zb, w_ref[:, c0:c0 + tf])
        ext_ref[FFN_HALO:FFN_HALO + tm, :] = up
        if sample:
            ext_ref[0:FFN_HALO, :] = jnp.zeros((FFN_HALO, tf), F32)
            up_ref[:, col0:col0 + tf] = up
        else:
            ext_ref[0:FFN_HALO, :] = jnp.where(first, 0.0, carry_ref[:, col0:col0 + tf])
            carry_ref[:, col0:col0 + tf] = up[tm - FFN_HALO:, :]
            up_ref[:, col0:col0 + tf] = up[tm - FFN_HALO:, :]
        s1 = ext_ref[pl.ds(FFN_HALO - 1, tm), :]
        s2 = ext_ref[pl.ds(FFN_HALO - 2, tm), :]
        if sample:
            s1 = jnp.where(tpos == 0, h1_ref[:, col0:col0 + tf], s1)
            s2 = jnp.where(tpos <= 1, h2_ref[:, col0:col0 + tf], s2)
        cw = cw_ref[:, col0:col0 + tf]
        return cw[0:1] * s2 + cw[1:2] * s1 + cw[2:3] * up + cb_ref[:, col0:col0 + tf]

    for c in range(dff // tf):
        hg = conv_half(wg_ref, c * tf, c * tf)
        hv = conv_half(wv_ref, c * tf, dff + c * tf)
        act_ref[:, c * tf:(c + 1) * tf] = (_gelu_tanh(hg) * hv).astype(BF16)
    xo_ref[...] = x + _rms(_dot(act_ref[...], wd_ref[...]), gpost_ref[...])


def _ffn(x, gpre, wg, wv, cw, cb, wd, gpost, tm, tf, seq=None, hist=None, nq=None):
    n, d = x.shape
    dff = wd.shape[0]
    sample = hist is not None
    const = lambda i: (0, 0)
    row = lambda i: (i, 0)
    single = pl.Buffered(1)
    in_specs = [pl.BlockSpec((tm, d), row), pl.BlockSpec((1, d), const),
                pl.BlockSpec(wg.shape, const, pipeline_mode=single),
                pl.BlockSpec(wv.shape, const, pipeline_mode=single),
                pl.BlockSpec(cw.shape, const), pl.BlockSpec(cb.shape, const),
                pl.BlockSpec(wd.shape, const, pipeline_mode=single), pl.BlockSpec((1, d), const)]
    args = [x, gpre, wg, wv, cw, cb, wd, gpost]
    scratch = [pltpu.VMEM((FFN_HALO + tm, tf), F32), pltpu.VMEM((tm, dff), BF16)]
    if sample:
        in_specs += [pl.BlockSpec((tm, 2 * dff), row)] * 2
        args += list(hist)
        up_rows, tiles_per_seq = tm, 1
    else:
        scratch.append(pltpu.VMEM((FFN_HALO, 2 * dff), F32))
        up_rows, tiles_per_seq, nq = FFN_HALO, seq // tm, tm
    return pl.pallas_call(
        functools.partial(_ffn_kernel, sample=sample, tiles_per_seq=tiles_per_seq, tf=tf, nq=nq),
        grid=(n // tm,),
        in_specs=in_specs,
        out_specs=[pl.BlockSpec((tm, d), row), pl.BlockSpec((up_rows, 2 * dff), row)],
        out_shape=[jax.ShapeDtypeStruct((n, d), F32),
                   jax.ShapeDtypeStruct((n // tm * up_rows, 2 * dff), F32)],
        scratch_shapes=scratch,
        compiler_params=_cparams("arbitrary"),
        name="ffn_sample" if sample else "ffn_prompt",
    )(*args)


def _bias(cond):
    return jnp.asarray(np.where(cond, NEG_BIG, 0.0).astype(np.float32), dtype=BF16)


def _onehot_rows(rows, nq):
    return jnp.asarray((np.arange(LANES)[None, :] == (np.arange(rows)[:, None] % nq)).astype(np.float32), dtype=BF16)


def _cmp_sel_table(n_rows, n_cmp, n_slc):
    s = np.arange(n_rows) * CMP_STRIDE
    e = s + CMP_LEN - 1
    js = np.arange(LANES) * SEL_LEN
    je = js + SEL_LEN - 1
    m = (s[:, None] <= je[None, :]) & (e[:, None] >= js[None, :])
    m &= (np.arange(n_rows)[:, None] < n_cmp) & (np.arange(LANES)[None, :] < n_slc)
    return m.astype(np.float32)


def _prompt_tables(seq, tk):
    nq = Q_BLOCK
    n_rows = seq // CMP_STRIDE
    lane = np.arange(LANES)
    n = np.arange(n_rows)
    blk = np.arange(seq // nq)
    cmpt = _bias(CMP_STRIDE * n[None, :, None] + CMP_LEN - 1 > nq * blk[:, None, None] + lane[None, None, :])
    key = np.arange(seq).reshape(seq // tk, tk)
    selt = _bias(key[:, :, None] // SEL_LEN == (lane % HEAD_DIM)[None, None, :])
    i = np.arange(tk)
    d = np.arange(tk // nq + 1)
    cast = _bias((i[None, :, None] > nq * d[:, None, None] + lane[None, None, :]) & (d < tk // nq)[:, None, None])
    span = np.arange(WINDOW + nq)
    w = np.arange(WINDOW // nq + 1)
    dt = (nq * w[:, None, None] + lane[None, None, :]) - span[None, :, None]
    wint = _bias((dt < 0) | (dt > WINDOW))
    n_cmp = (seq - CMP_LEN) // CMP_STRIDE + 1
    selmap_t = jnp.asarray(_cmp_sel_table(n_rows, n_cmp, seq // SEL_LEN).T, dtype=BF16)
    return cmpt, selt, cast, wint, _onehot_rows(N_HEADS * nq, nq), selmap_t


def _sample_tables(past, page, nq, wb):
    n_rows = past // CMP_STRIDE
    lane = np.arange(LANES)
    n = np.arange(n_rows)
    qvalid = lane < nq
    cmpt = _bias((CMP_STRIDE * n[:, None] + CMP_LEN - 1 > past + lane[None, :]) & qvalid[None, :])
    key = np.arange(past).reshape(past // page, 1, page)
    selt = _bias(key // SEL_LEN == lane[None, :, None])
    i = np.arange(page)
    new_sel = (past + i[:, None]) // SEL_LEN == lane[None, :]
    new_causal = (i[:, None] > lane[None, :]) & qvalid[None, :]
    newt = _bias(np.concatenate([new_sel, new_causal], axis=1))
    dt = (lane[:, None] + wb) - np.arange(wb)[None, :]
    wint = _bias(((dt < 0) | (dt > WINDOW)) & qvalid[:, None])
    n_cmp = (past + nq - CMP_LEN) // CMP_STRIDE + 1
    selmap = jnp.asarray(_cmp_sel_table(n_rows, n_cmp, -(-(past + nq) // SEL_LEN)), dtype=BF16)
    return cmpt, selt, newt, wint, _onehot_rows(N_HEADS * nq, nq), selmap


def _phi_weights(w):
    depth = w.shape[0]
    wj = w.reshape(depth, CMP_LEN, HEAD_DIM, HEAD_DIM)
    eye = jnp.eye(N_KV, dtype=w.dtype)
    bd = jnp.einsum("gh,ljde->ljgdhe", eye, wj).reshape(depth, CMP_LEN, KV_W, KV_W)
    return jnp.concatenate([bd[:, :CMP_STRIDE], bd[:, CMP_STRIDE:]], axis=-1).astype(BF16)


def kernel(x_prompt, x_sample, cache_kv, state_kv_win, state_conv, state_ffn, page_table, norm_mix_pre,
           norm_mix_post, norm_ffn_pre, norm_ffn_post, w_in, w_phi_k, b_phi_k, w_phi_v, b_phi_v, w_attn_out,
           conv_w, conv_b, conv_ln_g, conv_ln_b, w_conv_out, w_out, w_up, ffn_conv_w, ffn_conv_b, w_down):
    batch, seq, d = x_prompt.shape
    db, nq, _ = x_sample.shape
    depth, n_phys, page = cache_kv.shape[:3]
    n_pages = page_table.shape[1]
    past = n_pages * page
    wb = state_kv_win.shape[2]
    cc = conv_w.shape[2]
    dff = w_down.shape[1]
    assert seq % Q_BLOCK == 0 and nq == SUBLANES and page == LANES and wb == WINDOW

    tm = 512
    tm_f = 512
    tm_s = 128
    tf = 256
    tk = 512
    ns_attn = 2
    assert dff % tf == 0 and seq % tk == 0 and tk % Q_BLOCK == 0 and WINDOW % Q_BLOCK == 0

    c1 = ATTN_W + 6 * KV_W
    c2 = c1 + 3 * N_HEADS
    c3 = c2 + 2 * cc
    wq = w_in[:, :, :c1].astype(BF16)
    wgate = jnp.pad(w_in[:, :, c1:c2], ((0, 0), (0, 0), (0, GATE_PAD - 3 * N_HEADS))).astype(BF16)
    wu = w_in[:, :, c2:c3].astype(BF16)
    wm = w_in[:, :, c3:].astype(BF16)
    wao = w_attn_out.astype(BF16)
    wco = w_conv_out.astype(BF16)
    wout = w_out.astype(BF16)
    wup_g = w_up[:, :, :dff].astype(BF16)
    wup_v = w_up[:, :, dff:].astype(BF16)
    wdn = w_down.astype(BF16)
    wk_ab = _phi_weights(w_phi_k)
    wv_ab = _phi_weights(w_phi_v)
    bk2 = jnp.tile(b_phi_k, (1, N_KV)).reshape(depth, 1, KV_W)
    bv2 = jnp.tile(b_phi_v, (1, N_KV)).reshape(depth, 1, KV_W)

    tables_p = _prompt_tables(seq, tk)
    tables_s = _sample_tables(past, page, nq, wb)

    cache_t = cache_kv.transpose(0, 1, 3, 4, 5, 2).reshape(depth, n_phys, 4 * KV_W, page)
    wbuf_t = state_kv_win.transpose(0, 1, 3, 4, 5, 2).reshape(depth, db, 2 * KV_W, wb)
    row = lambda v: v.reshape(1, -1)

    xp = x_prompt.reshape(batch * seq, d)
    xs = x_sample.reshape(db * nq, d)
    outs = [[] for _ in range(8)]
    win_acc = jnp.zeros(wbuf_t.shape, F32)
    for l in range(depth):
        gpre, gpost = row(norm_mix_pre[l]), row(norm_mix_post[l])
        fpre, fpost = row(norm_ffn_pre[l]), row(norm_ffn_post[l])
        cb, lg, lb = row(conv_b[l]), row(conv_ln_g[l]), row(conv_ln_b[l])
        fcb = row(ffn_conv_b[l])

        q, kv, win, att_b, gates, u = _in_proj(xp, gpre, wq[l], wgate[l], wu[l], tm)
        kcmp, vcmp = _compress_prompt(kv, wk_ab[l], wv_ab[l], bk2[l], bv2[l], batch, seq)
        oa = _attn_prompt(q, gates, att_b, kcmp, vcmp, tables_p, batch, seq, tk)
        oc = _conv_prompt(u, conv_w[l], cb, lg, lb, seq, tm)
        xp = _merge(xp, gpre, oa, oc, wm[l], wao[l], wco[l], wout[l], gpost, tm)
        xp, up_tail = _ffn(xp, fpre, wup_g[l], wup_v[l], ffn_conv_w[l], fcb, wdn[l], fpost, tm_f, tf, seq=seq)
        outs[0].append(kv.reshape(batch, seq, 4, N_KV, HEAD_DIM))
        outs[2].append(win.reshape(batch, seq, 2, N_KV, HEAD_DIM)[:, seq - min(WINDOW, seq):])
        outs[4].append(u.reshape(batch, seq, cc)[:, seq - (CONV_WIDTH - 1):])
        outs[6].append(up_tail.reshape(batch, seq // tm_f, FFN_HALO, 2 * dff)[:, -1, FFN_HALO - (FFN_CONV_WIDTH - 1):])

        q, kv, win, _, gates, u = _in_proj(xs, gpre, wq[l], wgate[l], wu[l], tm)
        oa, win_acc = _attn_sample(page_table, q.reshape(db, nq, ATTN_W), gates.reshape(db, nq, GATE_PAD),
                          kv.reshape(db, nq, 4 * KV_W), win.reshape(db, nq, 2 * KV_W), wbuf_t, cache_t, l,
                          wk_ab[l], wv_ab[l], bk2[l], bv2[l], tables_s, ns_attn, win_acc)
        ext = jnp.concatenate([state_conv[l], u.reshape(db, nq, cc)], axis=1)
        oc = _conv_sample(ext, conv_w[l], cb, lg, lb, nq, 16)
        xs = _merge(xs, gpre, oa.reshape(db * nq, ATTN_W), oc.reshape(db * nq, cc),
                    wm[l], wao[l], wco[l], wout[l], gpost, tm)
        hist = state_ffn[l]
        h2 = jnp.pad(hist, ((0, 0), (0, nq - (FFN_CONV_WIDTH - 1)), (0, 0))).reshape(db * nq, 2 * dff)
        h1 = jnp.pad(hist[:, 1:], ((0, 0), (0, nq - 1), (0, 0))).reshape(db * nq, 2 * dff)
        xs, up_s = _ffn(xs, fpre, wup_g[l], wup_v[l], ffn_conv_w[l], fcb, wdn[l], fpost, tm_s, tf,
                        hist=(h1, h2), nq=nq)
        outs[1].append(kv.reshape(db, nq, 4, N_KV, HEAD_DIM))
        outs[5].append(ext[:, nq:])
        outs[7].append(up_s.reshape(db, nq, 2 * dff)[:, nq - (FFN_CONV_WIDTH - 1):])

    return (xp.reshape(batch, seq, d), xs.reshape(db, nq, d),
            jnp.stack(outs[0]), jnp.stack(outs[1]), jnp.stack(outs[2]),
            win_acc.reshape(depth, db, 2, N_KV, HEAD_DIM, wb).transpose(0, 1, 5, 2, 3, 4),
            jnp.stack(outs[4]), jnp.stack(outs[5]), jnp.stack(outs[6]), jnp.stack(outs[7]))
```
